```python
import math
import jax, jax.numpy as jnp
from jax import lax
import numpy as np

D_MODEL = 1024
BATCH = 2
SEQ = 8192
DEPTH = 2

MIX_WIDTH = D_MODEL
DA_HEADS = 4
DA_QK_DIM = 64
DA_V_DIM = 2 * DA_QK_DIM
DA_WIDTH = DA_HEADS * DA_V_DIM
RET_HEADS = 4
RET_QK_DIM = 64
RET_V_DIM = 2 * RET_QK_DIM
RET_WIDTH = RET_HEADS * RET_V_DIM
Q_BLOCK = 128
RET_CHUNK = 128
EPS = 1e-6

DA_Q_COLS = DA_HEADS * 2 * DA_QK_DIM
DA_K_COLS = DA_HEADS * 2 * DA_QK_DIM
DA_V_COLS = DA_WIDTH
DA_G_COLS = DA_WIDTH
RET_Q_COLS = RET_HEADS * RET_QK_DIM
RET_K_COLS = RET_HEADS * RET_QK_DIM
RET_V_COLS = RET_WIDTH
RET_G_COLS = RET_WIDTH
IN_WIDTH = DA_Q_COLS + DA_K_COLS + DA_V_COLS + DA_G_COLS + RET_Q_COLS + RET_K_COLS + RET_V_COLS + RET_G_COLS

kernel_name = "hymba_diffattn_retention_encoder"


def _rms(x, w=None):
    xf = x.astype(jnp.float32)
    y = xf * lax.rsqrt(jnp.mean(xf * xf, axis=-1, keepdims=True) + EPS)
    if w is not None:
        y = y * w.astype(jnp.float32)
    return y.astype(x.dtype)


def _alibi_slopes(n_heads):
    return jnp.asarray([2.0 ** (-8.0 * (i + 1) / n_heads) for i in range(n_heads)], dtype=jnp.float32)


def _diff_attention(q, k, v, lam, slopes):
    b, s = q.shape[0], q.shape[1]
    q = jnp.transpose(q, (0, 2, 3, 1, 4))
    k = jnp.transpose(k, (0, 2, 3, 1, 4))
    v = jnp.transpose(v, (0, 2, 1, 3))
    scale = DA_QK_DIM ** -0.5
    key_pos = jnp.arange(s)

    def block(i):
        start = i * Q_BLOCK
        qb = lax.dynamic_slice_in_dim(q, start, Q_BLOCK, axis=3)
        sc = jnp.einsum('bhmqd,bhmkd->bhmqk', qb, k).astype(jnp.float32) * scale
        qpos = start + jnp.arange(Q_BLOCK)
        dist = jnp.abs(qpos[:, None] - key_pos[None, :]).astype(jnp.float32)
        sc = sc - (slopes[:, None, None] * dist)[None, :, None]
        p = jax.nn.softmax(sc, axis=-1)
        w = p[:, :, 0] - lam * p[:, :, 1]
        return jnp.einsum('bhqk,bhkd->bhqd', w.astype(v.dtype), v)

    out = lax.map(block, jnp.arange(s // Q_BLOCK))
    return jnp.transpose(out, (1, 0, 3, 2, 4)).reshape(b, s, DA_HEADS, DA_V_DIM)


def _retention_dir(q, k, v, log_g, inclusive):
    b, h, s, dk = q.shape
    dv = v.shape[-1]
    nc = s // RET_CHUNK

    def chunks(t):
        return jnp.moveaxis(t.reshape(b, h, nc, RET_CHUNK, t.shape[-1]), 2, 0)

    n = jnp.arange(RET_CHUNK)
    diff = n[:, None] - n[None, :]
    mask = diff >= 0 if inclusive else diff > 0
    dmat = jnp.where(mask[None], jnp.exp(log_g[:, None, None] * jnp.maximum(diff, 0).astype(jnp.float32)), 0.0)
    xi = jnp.exp(log_g[:, None] * (n + 1).astype(jnp.float32))
    zeta = jnp.exp(log_g[:, None] * (RET_CHUNK - 1 - n).astype(jnp.float32))
    chunk_decay = jnp.exp(log_g * RET_CHUNK)

    def step(state, qkv):
        qc, kc, vc = qkv
        inner = jnp.einsum('bhqk,bhkv->bhqv', jnp.einsum('bhqd,bhkd->bhqk', qc, kc) * dmat, vc)
        cross = jnp.einsum('bhqd,bhdv->bhqv', qc * xi[..., None], state)
        new_state = chunk_decay[:, None, None] * state + jnp.einsum('bhkd,bhkv->bhdv', kc * zeta[..., None], vc)
        return new_state, inner + cross

    state0 = jnp.zeros((b, h, dk, dv), jnp.float32)
    _, out = lax.scan(step, state0, (chunks(q), chunks(k), chunks(v)))
    return jnp.moveaxis(out, 0, 2).reshape(b, h, s, dv)


def _bidirectional_retention(q, k, v, decay_fwd, decay_bwd):
    qf = jnp.transpose(q, (0, 2, 1, 3)).astype(jnp.float32)
    kf = jnp.transpose(k, (0, 2, 1, 3)).astype(jnp.float32) * (RET_QK_DIM ** -0.5)
    vf = jnp.transpose(v, (0, 2, 1, 3)).astype(jnp.float32)
    lg_f = jax.nn.log_sigmoid(decay_fwd.astype(jnp.float32))
    lg_b = jax.nn.log_sigmoid(decay_bwd.astype(jnp.float32))
    fwd = _retention_dir(qf, kf, vf, lg_f, True)
    flip = lambda t: jnp.flip(t, axis=2)
    bwd = flip(_retention_dir(flip(qf), flip(kf), flip(vf), lg_b, False))
    out = _rms(fwd + bwd)
    return jnp.transpose(out, (0, 2, 1, 3)).astype(q.dtype)


def setup_inputs(seed: int = 0) -> dict:
    key = jax.random.key(seed)
    ks = jax.random.split(key, 14)
    x = jax.random.normal(ks[0], (BATCH, SEQ, D_MODEL), jnp.float32)
    norm_w = 1.0 + 0.02 * jax.random.normal(ks[1], (DEPTH, D_MODEL), jnp.float32)
    w_in = jax.random.normal(ks[2], (DEPTH, D_MODEL, IN_WIDTH), jnp.float32) * D_MODEL ** -0.5
    q_norm_w = 1.0 + 0.02 * jax.random.normal(ks[3], (DEPTH, DA_QK_DIM), jnp.float32)
    k_norm_w = 1.0 + 0.02 * jax.random.normal(ks[4], (DEPTH, DA_QK_DIM), jnp.float32)
    lambda_q1 = 0.1 * jax.random.normal(ks[5], (DEPTH, DA_QK_DIM), jnp.float32)
    lambda_k1 = 0.1 * jax.random.normal(ks[6], (DEPTH, DA_QK_DIM), jnp.float32)
    lambda_q2 = 0.1 * jax.random.normal(ks[7], (DEPTH, DA_QK_DIM), jnp.float32)
    lambda_k2 = 0.1 * jax.random.normal(ks[8], (DEPTH, DA_QK_DIM), jnp.float32)
    subln_w = 1.0 + 0.02 * jax.random.normal(ks[9], (DEPTH, DA_V_DIM), jnp.float32)
    gamma = 1.0 - 2.0 ** (-5.0 - jnp.arange(RET_HEADS, dtype=jnp.float32))
    base_logit = jnp.log(gamma) - jnp.log1p(-gamma)
    ret_decay_fwd = base_logit[None] + 0.1 * jax.random.normal(ks[10], (DEPTH, RET_HEADS), jnp.float32)
    ret_decay_bwd = base_logit[None] + 0.1 * jax.random.normal(ks[11], (DEPTH, RET_HEADS), jnp.float32)
    w_out = jax.random.normal(ks[12], (DEPTH, MIX_WIDTH, D_MODEL), jnp.float32) * (MIX_WIDTH ** -0.5) / math.sqrt(2.0 * DEPTH)
    return {"x": x, "norm_w": norm_w, "w_in": w_in, "q_norm_w": q_norm_w, "k_norm_w": k_norm_w,
            "lambda_q1": lambda_q1, "lambda_k1": lambda_k1, "lambda_q2": lambda_q2, "lambda_k2": lambda_k2,
            "subln_w": subln_w, "ret_decay_fwd": ret_decay_fwd, "ret_decay_bwd": ret_decay_bwd, "w_out": w_out}


def reference(x, norm_w, w_in, q_norm_w, k_norm_w, lambda_q1, lambda_k1, lambda_q2, lambda_k2,
              subln_w, ret_decay_fwd, ret_decay_bwd, w_out):
    b, s, _ = x.shape
    slopes = _alibi_slopes(DA_HEADS)
    sizes = [DA_Q_COLS, DA_K_COLS, DA_V_COLS, DA_G_COLS, RET_Q_COLS, RET_K_COLS, RET_V_COLS, RET_G_COLS]
    offsets = [int(o) for o in np.cumsum(sizes)[:-1]]
    for l in range(DEPTH):
        xn = _rms(x, norm_w[l])
        h = jnp.einsum('bsd,de->bse', xn, w_in[l])
        qa, ka, va, ga, qr, kr, vr, gr = jnp.split(h, offsets, axis=-1)

        lambda_init = 0.8 - 0.6 * math.exp(-0.3 * l)
        lam = (jnp.exp(jnp.sum(lambda_q1[l].astype(jnp.float32) * lambda_k1[l].astype(jnp.float32)))
               - jnp.exp(jnp.sum(lambda_q2[l].astype(jnp.float32) * lambda_k2[l].astype(jnp.float32)))
               + lambda_init)
        qa = _rms(qa.reshape(b, s, DA_HEADS, 2, DA_QK_DIM), q_norm_w[l])
        ka = _rms(ka.reshape(b, s, DA_HEADS, 2, DA_QK_DIM), k_norm_w[l])
        va = va.reshape(b, s, DA_HEADS, DA_V_DIM)
        oa = _diff_attention(qa, ka, va, lam, slopes)
        oa = (_rms(oa, subln_w[l]) * (1.0 - lambda_init)).reshape(b, s, DA_WIDTH)

        qr = qr.reshape(b, s, RET_HEADS, RET_QK_DIM)
        kr = kr.reshape(b, s, RET_HEADS, RET_QK_DIM)
        vr = vr.reshape(b, s, RET_HEADS, RET_V_DIM)
        orr = _bidirectional_retention(qr, kr, vr, ret_decay_fwd[l], ret_decay_bwd[l]).reshape(b, s, RET_WIDTH)

        mixed = jnp.concatenate([oa * jax.nn.silu(ga), orr * jax.nn.silu(gr)], axis=-1)
        x = x + jnp.einsum('bse,ed->bsd', mixed, w_out[l])
    return x
```

```python
import functools
import math

import jax
import jax.numpy as jnp
from jax import lax
from jax.experimental import pallas as pl
from jax.experimental.pallas import tpu as pltpu

D_MODEL = 1024
DA_HEADS = 4
DA_QK_DIM = 64
DA_V_DIM = 128
DA_WIDTH = DA_HEADS * DA_V_DIM
RET_HEADS = 4
RET_QK_DIM = 64
RET_V_DIM = 128
RET_WIDTH = RET_HEADS * RET_V_DIM
IN_WIDTH = 3584
EPS = 1e-6

_QA, _KA, _VA, _GA, _QR, _KR, _VR, _GR = 0, 512, 1024, 1536, 2048, 2304, 2560, 3072

_POS_SPLIT = 256
_KAUG_LANES = 256
_VT_ROWS = 144
_VMEM_LIMIT_BYTES = 56 * 1024 * 1024

_f32 = jnp.float32
_bf16 = jnp.bfloat16


def _alibi_slope(head):
    return 2.0 ** (-8.0 * (head + 1) / DA_HEADS)


def _inproj_body(x_ref, nw_ref, w_ref, qnw_ref, knw_ref, gsum_ref,
                 qT_ref, kaug_ref, vT_ref, gate_ref, qr_ref, kr_ref, vr_ref, *, tm, tiles_per_seq):
    x = x_ref[...]
    ms = jnp.mean(x * x, axis=-1, keepdims=True)
    xn = (x * lax.rsqrt(ms + EPS) * nw_ref[...]).astype(_bf16)
    h = jnp.dot(xn, w_ref[...], preferred_element_type=_f32)

    gsum = gsum_ref[...]

    def group_rms(y, w):
        y2 = y * y
        hi = y2.astype(_bf16)
        lo = (y2 - hi.astype(_f32)).astype(_bf16)
        ss = (jnp.dot(hi, gsum, preferred_element_type=_f32)
              + jnp.dot(lo, gsum, preferred_element_type=_f32))
        return y * lax.rsqrt(ss * (1.0 / DA_QK_DIM) + EPS) * w

    st = pl.program_id(0) % tiles_per_seq
    row = lax.broadcasted_iota(jnp.int32, (tm, 128), 0) + st * tm
    lane = lax.broadcasted_iota(jnp.int32, (tm, 128), 1)
    pos_lo = (row % _POS_SPLIT).astype(_f32)
    pos_hi = ((row // _POS_SPLIT) * _POS_SPLIT).astype(_f32)
    vrow = lax.broadcasted_iota(jnp.int32, (_VT_ROWS - DA_V_DIM, tm), 0)
    ones_rows = jnp.where(vrow == 0, 1.0, 0.0).astype(_bf16)

    scale = DA_QK_DIM ** -0.5
    for hd in range(DA_HEADS):
        c = 128 * hd
        q = group_rms(h[:, _QA + c:_QA + c + 128], qnw_ref[...]) * scale
        qT_ref[0, hd] = q.T.astype(_bf16)
        k = group_rms(h[:, _KA + c:_KA + c + 128], knw_ref[...])
        kaug_ref[0, hd, :, 0:128] = k.astype(_bf16)
        slope = _alibi_slope(hd)
        kpos = jnp.where(lane == 0, slope * pos_lo,
                         jnp.where(lane == 1, slope * pos_hi,
                                   jnp.where(lane < 4, 1.0, 0.0)))
        kaug_ref[0, hd, :, 128:256] = kpos.astype(_bf16)
        v = h[:, _VA + c:_VA + c + 128]
        vT_ref[0, hd, 0, 0:DA_V_DIM, :] = v.T.astype(_bf16)
        vT_ref[0, hd, 0, DA_V_DIM:_VT_ROWS, :] = ones_rows

    ga = h[:, _GA:_GA + DA_WIDTH]
    gr = h[:, _GR:_GR + RET_WIDTH]
    gate_ref[:, 0:DA_WIDTH] = (ga * jax.nn.sigmoid(ga)).astype(_bf16)
    gate_ref[:, DA_WIDTH:DA_WIDTH + RET_WIDTH] = (gr * jax.nn.sigmoid(gr)).astype(_bf16)
    qr_ref[...] = h[:, _QR:_QR + 256].astype(_bf16)
    kr_ref[...] = (h[:, _KR:_KR + 256] * (RET_QK_DIM ** -0.5)).astype(_bf16)
    vr_ref[...] = h[:, _VR:_VR + RET_WIDTH].astype(_bf16)


def _inproj(x2d, norm_w, w_in_bf16, qnw2, knw2, gsum, *, batch, seq, tm):
    rows = batch * seq
    tiles_per_seq = seq // tm
    nk = seq // tm
    full = lambda shape: pl.BlockSpec(shape, lambda i: (0,) * len(shape))
    b_of = lambda i: i // tiles_per_seq
    s_of = lambda i: i % tiles_per_seq
    out_shape = (
        jax.ShapeDtypeStruct((batch, DA_HEADS, 128, seq), _bf16),
        jax.ShapeDtypeStruct((batch, DA_HEADS, seq, _KAUG_LANES), _bf16),
        jax.ShapeDtypeStruct((batch, DA_HEADS, nk, _VT_ROWS, tm), _bf16),
        jax.ShapeDtypeStruct((rows, DA_WIDTH + RET_WIDTH), _bf16),
        jax.ShapeDtypeStruct((rows, 256), _bf16),
        jax.ShapeDtypeStruct((rows, 256), _bf16),
        jax.ShapeDtypeStruct((rows, RET_WIDTH), _bf16),
    )
    out_specs = (
        pl.BlockSpec((1, DA_HEADS, 128, tm), lambda i: (b_of(i), 0, 0, s_of(i))),
        pl.BlockSpec((1, DA_HEADS, tm, _KAUG_LANES), lambda i: (b_of(i), 0, s_of(i), 0)),
        pl.BlockSpec((1, DA_HEADS, 1, _VT_ROWS, tm), lambda i: (b_of(i), 0, s_of(i), 0, 0)),
        pl.BlockSpec((tm, DA_WIDTH + RET_WIDTH), lambda i: (i, 0)),
        pl.BlockSpec((tm, 256), lambda i: (i, 0)),
        pl.BlockSpec((tm, 256), lambda i: (i, 0)),
        pl.BlockSpec((tm, RET_WIDTH), lambda i: (i, 0)),
    )
    return pl.pallas_call(
        functools.partial(_inproj_body, tm=tm, tiles_per_seq=tiles_per_seq),
        grid=(rows // tm,),
        in_specs=[
            pl.BlockSpec((tm, D_MODEL), lambda i: (i, 0)),
            full((1, D_MODEL)),
            full((D_MODEL, IN_WIDTH)),
            full((1, 128)),
            full((1, 128)),
            full((128, 128)),
        ],
        out_specs=out_specs,
        out_shape=out_shape,
        compiler_params=pltpu.CompilerParams(
            dimension_semantics=("arbitrary",), vmem_limit_bytes=_VMEM_LIMIT_BYTES),
        name="inproj",
    )(x2d, norm_w, w_in_bf16, qnw2, knw2, gsum)


def _attn_body(qT_ref, kaug_ref, vT_ref, lamv_ref, o_ref, qw_ref, acc_ref, m_ref, d_ref,
               *, T, nk, lambda_init):
    hd = pl.program_id(1)
    qi = pl.program_id(2)
    slope = jnp.where(hd == 0, _alibi_slope(0),
                      jnp.where(hd == 1, _alibi_slope(1),
                                jnp.where(hd == 2, _alibi_slope(2), _alibi_slope(3)))).astype(_f32)

    @pl.when(qi == 0)
    def _():
        jj = lax.broadcasted_iota(jnp.int32, (T, T), 0)
        ii = lax.broadcasted_iota(jnp.int32, (T, T), 1)
        d_ref[...] = (-2.0 * slope) * jnp.maximum(jj - ii, 0).astype(_f32)

    qt = qT_ref[0, 0]
    prow = lax.broadcasted_iota(jnp.int32, (16, T), 0)
    ipos = lax.broadcasted_iota(jnp.int32, (16, T), 1) + qi * T
    ilo = (ipos % _POS_SPLIT).astype(_f32)
    ihi = ((ipos // _POS_SPLIT) * _POS_SPLIT).astype(_f32)
    qpos = jnp.where(prow < 2, 1.0,
                     jnp.where(prow == 2, -slope * ilo,
                               jnp.where(prow == 3, -slope * ihi, 0.0)))
    zeros64 = jnp.zeros((64, T), _bf16)
    for side in range(2):
        aug = (qpos if side == 0 else -qpos).astype(_bf16)
        for mp in range(2):
            idx = 2 * side + mp
            qw_ref[idx, 0:64, :] = qt[0:64] if mp == 0 else zeros64
            qw_ref[idx, 64:128, :] = zeros64 if mp == 0 else qt[64:128]
            qw_ref[idx, 128:144, :] = aug
            qw_ref[idx, 144:_KAUG_LANES, :] = jnp.zeros((_KAUG_LANES - 144, T), _bf16)

    acc_ref[...] = jnp.zeros_like(acc_ref)
    m_ref[...] = jnp.full(m_ref.shape, -1e30, _f32)

    def step(kt, side, diag):
        ktile = kaug_ref[0, 0, pl.ds(pl.multiple_of(kt * T, T), T), :]
        vtile = vT_ref[0, 0, kt]
        for mp in range(2):
            s = jnp.dot(ktile, qw_ref[2 * side + mp], preferred_element_type=_f32)
            if diag:
                s = s + d_ref[...]
            m_old = m_ref[mp]
            m_new = jnp.maximum(m_old, jnp.max(s, axis=0, keepdims=True))
            alpha = jnp.exp(m_old - m_new)
            p = jnp.exp(s - m_new).astype(_bf16)
            pv = jnp.dot(vtile, p, preferred_element_type=_f32)
            acc_ref[mp] = acc_ref[mp] * alpha + pv
            m_ref[mp] = m_new

    def before(kt, carry):
        step(kt, 0, False)
        return carry

    def after(kt, carry):
        step(kt, 1, False)
        return carry

    lax.fori_loop(0, qi, before, 0)
    step(qi, 0, True)
    lax.fori_loop(qi + 1, nk, after, 0)

    lv = lamv_ref[...]
    lam = (jnp.exp(jnp.sum(lv[0:1] * lv[1:2], axis=-1, keepdims=True))
           - jnp.exp(jnp.sum(lv[2:3] * lv[3:4], axis=-1, keepdims=True)) + lambda_init)
    a0 = acc_ref[0]
    a1 = acc_ref[1]
    o = a0[0:DA_V_DIM] / a0[DA_V_DIM:DA_V_DIM + 1] - lam * (a1[0:DA_V_DIM] / a1[DA_V_DIM:DA_V_DIM + 1])
    o_ref[0] = o.T


def _attn(qT, kaug, vT, lamv, *, batch, seq, T, lambda_init):
    nk = seq // T
    return pl.pallas_call(
        functools.partial(_attn_body, T=T, nk=nk, lambda_init=lambda_init),
        grid=(batch, DA_HEADS, nk),
        in_specs=[
            pl.BlockSpec((1, 1, 128, T), lambda b, h, q: (b, h, 0, q)),
            pl.BlockSpec((1, 1, seq, _KAUG_LANES), lambda b, h, q: (b, h, 0, 0)),
            pl.BlockSpec((1, 1, nk, _VT_ROWS, T), lambda b, h, q: (b, h, 0, 0, 0)),
            pl.BlockSpec((4, DA_QK_DIM), lambda b, h, q: (0, 0)),
        ],
        out_specs=pl.BlockSpec((1, T, DA_V_DIM), lambda b, h, q: (b, q, h)),
        out_shape=jax.ShapeDtypeStruct((batch, seq, DA_WIDTH), _f32),
        scratch_shapes=[
            pltpu.VMEM((4, _KAUG_LANES, T), _bf16),
            pltpu.VMEM((2, _VT_ROWS, T), _f32),
            pltpu.VMEM((2, 1, T), _f32),
            pltpu.VMEM((T, T), _f32),
        ],
        compiler_params=pltpu.CompilerParams(
            dimension_semantics=("arbitrary", "arbitrary", "arbitrary"),
            vmem_limit_bytes=_VMEM_LIMIT_BYTES),
        name="diffattn",
    )(qT, kaug, vT, lamv)


def _log_sigmoid(x):
    return jnp.minimum(x, 0.0) - jnp.log1p(jnp.exp(-jnp.abs(x)))


def _ret_body(q_ref, k_ref, v_ref, dec_ref, o_ref, dm_ref, kvf_ref, kvb_ref, sf_ref, sb_ref, *, C, nc):
    hd = pl.program_id(1)
    lane = lax.broadcasted_iota(jnp.int32, (1, 128), 1)
    hmask = (lane // RET_QK_DIM) == (hd % 2)

    lg = _log_sigmoid(dec_ref[...])
    sel = lane == hd
    lgf = jnp.sum(jnp.where(sel, lg[0:1], 0.0), axis=-1, keepdims=True)
    lgb = jnp.sum(jnp.where(sel, lg[1:2], 0.0), axis=-1, keepdims=True)

    n_col = lax.broadcasted_iota(jnp.int32, (C, 1), 0).astype(_f32)
    tt = lax.broadcasted_iota(jnp.int32, (C, C), 0)
    ss = lax.broadcasted_iota(jnp.int32, (C, C), 1)
    diff = (tt - ss).astype(_f32)
    dm_ref[...] = jnp.where(diff >= 0.0, jnp.exp(lgf * jnp.maximum(diff, 0.0)),
                            jnp.exp(lgb * jnp.maximum(-diff, 0.0)))
    zeta_f = jnp.exp(lgf * (C - 1.0 - n_col))
    xi_f = jnp.exp(lgf * (n_col + 1.0))
    zeta_b = jnp.exp(lgb * (n_col + 1.0))
    xi_b = jnp.exp(lgb * (C - 1.0 - n_col))
    decay_f = jnp.exp(lgf * float(C))
    decay_b = jnp.exp(lgb * float(C))

    def rows_of(c):
        return pl.ds(pl.multiple_of(c * C, C), C)

    def intra(c, carry):
        rows = rows_of(c)
        q = jnp.where(hmask, q_ref[0, rows, :], jnp.zeros((), _bf16))
        k = k_ref[0, rows, :]
        v = v_ref[0, rows, :]
        a = lax.dot_general(q, k, (((1,), (1,)), ((), ())), preferred_element_type=_f32)
        a = (a * dm_ref[...]).astype(_bf16)
        o_ref[0, rows, :] = jnp.dot(a, v, preferred_element_type=_f32)
        kf = (k.astype(_f32) * zeta_f).astype(_bf16)
        kb = (k.astype(_f32) * zeta_b).astype(_bf16)
        kvf_ref[c] = lax.dot_general(kf, v, (((0,), (0,)), ((), ())), preferred_element_type=_f32)
        kvb_ref[c] = lax.dot_general(kb, v, (((0,), (0,)), ((), ())), preferred_element_type=_f32)
        return carry

    lax.fori_loop(0, nc, intra, 0)

    def scan_f(c, state):
        sf_ref[c] = state.astype(_bf16)
        return decay_f * state + kvf_ref[c]

    def scan_b(i, state):
        c = nc - 1 - i
        sb_ref[c] = state.astype(_bf16)
        return decay_b * state + kvb_ref[c]

    zero_state = jnp.zeros((128, RET_V_DIM), _f32)
    lax.fori_loop(0, nc, scan_f, zero_state)
    lax.fori_loop(0, nc, scan_b, zero_state)

    def cross(c, carry):
        rows = rows_of(c)
        q = jnp.where(hmask, q_ref[0, rows, :], jnp.zeros((), _bf16)).astype(_f32)
        qf = (q * xi_f).astype(_bf16)
        qb = (q * xi_b).astype(_bf16)
        o_ref[0, rows, :] += (jnp.dot(qf, sf_ref[c], preferred_element_type=_f32)
                              + jnp.dot(qb, sb_ref[c], preferred_element_type=_f32))
        return carry

    lax.fori_loop(0, nc, cross, 0)


def _retention(qr, kr, vr, dec, *, batch, seq, C):
    nc = seq // C
    return pl.pallas_call(
        functools.partial(_ret_body, C=C, nc=nc),
        grid=(batch, RET_HEADS),
        in_specs=[
            pl.BlockSpec((1, seq, 128), lambda b, h: (b, 0, h // 2)),
            pl.BlockSpec((1, seq, 128), lambda b, h: (b, 0, h // 2)),
            pl.BlockSpec((1, seq, RET_V_DIM), lambda b, h: (b, 0, h)),
            pl.BlockSpec((2, 128), lambda b, h: (0, 0)),
        ],
        out_specs=pl.BlockSpec((1, seq, RET_V_DIM), lambda b, h: (b, 0, h)),
        out_shape=jax.ShapeDtypeStruct((batch, seq, RET_WIDTH), _f32),
        scratch_shapes=[
            pltpu.VMEM((C, C), _f32),
            pltpu.VMEM((nc, 128, RET_V_DIM), _f32),
            pltpu.VMEM((nc, 128, RET_V_DIM), _f32),
            pltpu.VMEM((nc, 128, RET_V_DIM), _bf16),
            pltpu.VMEM((nc, 128, RET_V_DIM), _bf16),
        ],
        compiler_params=pltpu.CompilerParams(
            dimension_semantics=("arbitrary", "arbitrary"), vmem_limit_bytes=_VMEM_LIMIT_BYTES),
        name="retention",
    )(qr, kr, vr, dec)


def _outproj_body(x_ref, oa_ref, or_ref, gate_ref, sw_ref, w_ref, o_ref, *, lambda_init):
    oa = oa_ref[...]
    orr = or_ref[...]
    pieces = []
    for hd in range(DA_HEADS):
        y = oa[:, 128 * hd:128 * hd + 128]
        y = y * lax.rsqrt(jnp.mean(y * y, axis=-1, keepdims=True) + EPS) * sw_ref[...]
        pieces.append(y * (1.0 - lambda_init))
    for hd in range(RET_HEADS):
        y = orr[:, 128 * hd:128 * hd + 128]
        pieces.append(y * lax.rsqrt(jnp.mean(y * y, axis=-1, keepdims=True) + EPS))
    mixed = jnp.concatenate(pieces, axis=-1) * gate_ref[...].astype(_f32)
    o_ref[...] = x_ref[...] + jnp.dot(mixed.astype(_bf16), w_ref[...], preferred_element_type=_f32)


def _outproj(x2d, oa2d, or2d, gates, subln_w, w_out_bf16, *, tm, lambda_init):
    rows = x2d.shape[0]
    return pl.pallas_call(
        functools.partial(_outproj_body, lambda_init=lambda_init),
        grid=(rows // tm,),
        in_specs=[
            pl.BlockSpec((tm, D_MODEL), lambda i: (i, 0)),
            pl.BlockSpec((tm, DA_WIDTH), lambda i: (i, 0)),
            pl.BlockSpec((tm, RET_WIDTH), lambda i: (i, 0)),
            pl.BlockSpec((tm, DA_WIDTH + RET_WIDTH), lambda i: (i, 0)),
            pl.BlockSpec((1, DA_V_DIM), lambda i: (0, 0)),
            pl.BlockSpec((DA_WIDTH + RET_WIDTH, D_MODEL), lambda i: (0, 0)),
        ],
        out_specs=pl.BlockSpec((tm, D_MODEL), lambda i: (i, 0)),
        out_shape=jax.ShapeDtypeStruct((rows, D_MODEL), _f32),
        compiler_params=pltpu.CompilerParams(
            dimension_semantics=("arbitrary",), vmem_limit_bytes=_VMEM_LIMIT_BYTES),
        name="outproj",
    )(x2d, oa2d, or2d, gates, subln_w, w_out_bf16)


def _tiles(seq):
    attn_tile = min(512, seq // 2)
    ret_chunk = min(256, seq // 2)
    return attn_tile, ret_chunk


def kernel(x, norm_w, w_in, q_norm_w, k_norm_w, lambda_q1, lambda_k1, lambda_q2, lambda_k2,
           subln_w, ret_decay_fwd, ret_decay_bwd, w_out):
    batch, seq, _ = x.shape
    depth = norm_w.shape[0]
    T, C = _tiles(seq)
    rows = batch * seq

    lane = jnp.arange(128)
    gsum = (lane[:, None] // DA_QK_DIM == lane[None, :] // DA_QK_DIM).astype(_bf16)

    x2d = x.reshape(rows, D_MODEL)
    for l in range(depth):
        lambda_init = 0.8 - 0.6 * math.exp(-0.3 * l)
        qnw2 = jnp.tile(q_norm_w[l], 2).reshape(1, 128)
        knw2 = jnp.tile(k_norm_w[l], 2).reshape(1, 128)
        qT, kaug, vT, gates, qr, kr, vr = _inproj(
            x2d, norm_w[l].reshape(1, D_MODEL), w_in[l].astype(_bf16), qnw2, knw2, gsum,
            batch=batch, seq=seq, tm=T)
        lamv = jnp.stack([lambda_q1[l], lambda_k1[l], lambda_q2[l], lambda_k2[l]]).astype(_f32)
        oa = _attn(qT, kaug, vT, lamv, batch=batch, seq=seq, T=T, lambda_init=lambda_init)
        dec = jnp.zeros((2, 128), _f32)
        dec = dec.at[0, :RET_HEADS].set(ret_decay_fwd[l]).at[1, :RET_HEADS].set(ret_decay_bwd[l])
        orr = _retention(qr.reshape(batch, seq, 256), kr.reshape(batch, seq, 256),
                         vr.reshape(batch, seq, RET_WIDTH), dec, batch=batch, seq=seq, C=C)
        x2d = _outproj(x2d, oa.reshape(rows, DA_WIDTH), orr.reshape(rows, RET_WIDTH), gates,
                       subln_w[l].reshape(1, DA_V_DIM), w_out[l].astype(_bf16),
                       tm=T, lambda_init=lambda_init)
    return x2d.reshape(batch, seq, D_MODEL)
```

```python
import functools
import math

import jax
import jax.numpy as jnp
from jax import lax
from jax.experimental import pallas as pl
from jax.experimental.pallas import tpu as pltpu

D_MODEL = 1024
DA_HEADS = 4
DA_QK_DIM = 64
DA_V_DIM = 128
DA_WIDTH = DA_HEADS * DA_V_DIM
RET_HEADS = 4
RET_QK_DIM = 64
RET_V_DIM = 128
RET_WIDTH = RET_HEADS * RET_V_DIM
IN_WIDTH = 3584
EPS = 1e-6

_QA, _KA, _VA, _GA, _QR, _KR, _VR, _GR = 0, 512, 1024, 1536, 2048, 2304, 2560, 3072

_POS_SPLIT = 256
_KAUG_LANES = 256
_VT_ROWS = 144
_VMEM_LIMIT_BYTES = 56 * 1024 * 1024
_MAX_FIXED_SHIFT = 40.0
_BOUND_SLACK = 1.02

_f32 = jnp.float32
_bf16 = jnp.bfloat16


def _alibi_slope(head):
    return 2.0 ** (-8.0 * (head + 1) / DA_HEADS)


def _inproj_body(x_ref, nw_ref, w_ref, qnw_ref, knw_ref, gsum_ref,
                 qT_ref, kaug_ref, vT_ref, gate_ref, qr_ref, kr_ref, vr_ref, *, tm, tiles_per_seq):
    x = x_ref[...]
    ms = jnp.mean(x * x, axis=-1, keepdims=True)
    xn = (x * lax.rsqrt(ms + EPS) * nw_ref[...]).astype(_bf16)
    h = jnp.dot(xn, w_ref[...], preferred_element_type=_f32)

    gsum = gsum_ref[...]

    def group_rms(y, w):
        y2 = y * y
        hi = y2.astype(_bf16)
        lo = (y2 - hi.astype(_f32)).astype(_bf16)
        ss = (jnp.dot(hi, gsum, preferred_element_type=_f32)
              + jnp.dot(lo, gsum, preferred_element_type=_f32))
        return y * lax.rsqrt(ss * (1.0 / DA_QK_DIM) + EPS) * w

    st = pl.program_id(0) % tiles_per_seq
    row = lax.broadcasted_iota(jnp.int32, (tm, 128), 0) + st * tm
    lane = lax.broadcasted_iota(jnp.int32, (tm, 128), 1)
    pos_lo = (row % _POS_SPLIT).astype(_f32)
    pos_hi = ((row // _POS_SPLIT) * _POS_SPLIT).astype(_f32)
    vrow = lax.broadcasted_iota(jnp.int32, (_VT_ROWS - DA_V_DIM, tm), 0)
    ones_rows = jnp.where(vrow == 0, 1.0, 0.0).astype(_bf16)

    scale = DA_QK_DIM ** -0.5
    for hd in range(DA_HEADS):
        c = 128 * hd
        q = group_rms(h[:, _QA + c:_QA + c + 128], qnw_ref[...]) * scale
        qT_ref[0, hd] = q.T.astype(_bf16)
        k = group_rms(h[:, _KA + c:_KA + c + 128], knw_ref[...])
        kaug_ref[0, hd, :, 0:128] = k.astype(_bf16)
        slope = _alibi_slope(hd)
        kpos = jnp.where(lane == 0, slope * pos_lo,
                         jnp.where(lane == 1, slope * pos_hi,
                                   jnp.where(lane < 5, 1.0, 0.0)))
        kaug_ref[0, hd, :, 128:256] = kpos.astype(_bf16)
        v = h[:, _VA + c:_VA + c + 128]
        vT_ref[0, hd, 0, 0:DA_V_DIM, :] = v.T.astype(_bf16)
        vT_ref[0, hd, 0, DA_V_DIM:_VT_ROWS, :] = ones_rows

    ga = h[:, _GA:_GA + DA_WIDTH]
    gr = h[:, _GR:_GR + RET_WIDTH]
    gate_ref[:, 0:DA_WIDTH] = (ga * jax.nn.sigmoid(ga)).astype(_bf16)
    gate_ref[:, DA_WIDTH:DA_WIDTH + RET_WIDTH] = (gr * jax.nn.sigmoid(gr)).astype(_bf16)
    qr_ref[...] = h[:, _QR:_QR + 256].astype(_bf16)
    kr_ref[...] = (h[:, _KR:_KR + 256] * (RET_QK_DIM ** -0.5)).astype(_bf16)
    vr_ref[...] = h[:, _VR:_VR + RET_WIDTH].astype(_bf16)


def _inproj(x2d, norm_w, w_in_bf16, qnw2, knw2, gsum, *, batch, seq, tm):
    rows = batch * seq
    tiles_per_seq = seq // tm
    nk = seq // tm
    full = lambda shape: pl.BlockSpec(shape, lambda i: (0,) * len(shape))
    b_of = lambda i: i // tiles_per_seq
    s_of = lambda i: i % tiles_per_seq
    out_shape = (
        jax.ShapeDtypeStruct((batch, DA_HEADS, 128, seq), _bf16),
        jax.ShapeDtypeStruct((batch, DA_HEADS, seq, _KAUG_LANES), _bf16),
        jax.ShapeDtypeStruct((batch, DA_HEADS, nk, _VT_ROWS, tm), _bf16),
        jax.ShapeDtypeStruct((rows, DA_WIDTH + RET_WIDTH), _bf16),
        jax.ShapeDtypeStruct((rows, 256), _bf16),
        jax.ShapeDtypeStruct((rows, 256), _bf16),
        jax.ShapeDtypeStruct((rows, RET_WIDTH), _bf16),
    )
    out_specs = (
        pl.BlockSpec((1, DA_HEADS, 128, tm), lambda i: (b_of(i), 0, 0, s_of(i))),
        pl.BlockSpec((1, DA_HEADS, tm, _KAUG_LANES), lambda i: (b_of(i), 0, s_of(i), 0)),
        pl.BlockSpec((1, DA_HEADS, 1, _VT_ROWS, tm), lambda i: (b_of(i), 0, s_of(i), 0, 0)),
        pl.BlockSpec((tm, DA_WIDTH + RET_WIDTH), lambda i: (i, 0)),
        pl.BlockSpec((tm, 256), lambda i: (i, 0)),
        pl.BlockSpec((tm, 256), lambda i: (i, 0)),
        pl.BlockSpec((tm, RET_WIDTH), lambda i: (i, 0)),
    )
    return pl.pallas_call(
        functools.partial(_inproj_body, tm=tm, tiles_per_seq=tiles_per_seq),
        grid=(rows // tm,),
        in_specs=[
            pl.BlockSpec((tm, D_MODEL), lambda i: (i, 0)),
            full((1, D_MODEL)),
            full((D_MODEL, IN_WIDTH)),
            full((1, 128)),
            full((1, 128)),
            full((128, 128)),
        ],
        out_specs=out_specs,
        out_shape=out_shape,
        compiler_params=pltpu.CompilerParams(
            dimension_semantics=("arbitrary",), vmem_limit_bytes=_VMEM_LIMIT_BYTES),
        name="inproj",
    )(x2d, norm_w, w_in_bf16, qnw2, knw2, gsum)


def _attn_setup(qT_ref, shift, qw_ref, acc_ref, d_ref, *, T):
    hd = pl.program_id(1)
    qi = pl.program_id(2)
    slope = jnp.where(hd == 0, _alibi_slope(0),
                      jnp.where(hd == 1, _alibi_slope(1),
                                jnp.where(hd == 2, _alibi_slope(2), _alibi_slope(3)))).astype(_f32)

    @pl.when(qi == 0)
    def _():
        jj = lax.broadcasted_iota(jnp.int32, (T, T), 0)
        ii = lax.broadcasted_iota(jnp.int32, (T, T), 1)
        d_ref[...] = (-2.0 * slope) * jnp.maximum(jj - ii, 0).astype(_f32)

    qt = qT_ref[0, 0]
    prow = lax.broadcasted_iota(jnp.int32, (16, T), 0)
    ipos = lax.broadcasted_iota(jnp.int32, (16, T), 1) + qi * T
    ilo = (ipos % _POS_SPLIT).astype(_f32)
    ihi = ((ipos // _POS_SPLIT) * _POS_SPLIT).astype(_f32)
    qpos = jnp.where(prow < 2, 1.0,
                     jnp.where(prow == 2, -slope * ilo,
                               jnp.where(prow == 3, -slope * ihi, 0.0)))
    shift_row = jnp.where(prow == 4, shift, 0.0)
    zeros64 = jnp.zeros((64, T), _bf16)
    for side in range(2):
        aug = ((qpos if side == 0 else -qpos) + shift_row).astype(_bf16)
        for mp in range(2):
            idx = 2 * side + mp
            qw_ref[idx, 0:64, :] = qt[0:64] if mp == 0 else zeros64
            qw_ref[idx, 64:128, :] = zeros64 if mp == 0 else qt[64:128]
            qw_ref[idx, 128:144, :] = aug
            qw_ref[idx, 144:_KAUG_LANES, :] = jnp.zeros((_KAUG_LANES - 144, T), _bf16)

    acc_ref[...] = jnp.zeros_like(acc_ref)


def _attn_finish(lamv_ref, acc_ref, o_ref, *, lambda_init):
    lv = lamv_ref[...]
    lam = (jnp.exp(jnp.sum(lv[0:1] * lv[1:2], axis=-1, keepdims=True))
           - jnp.exp(jnp.sum(lv[2:3] * lv[3:4], axis=-1, keepdims=True)) + lambda_init)
    a0 = acc_ref[0]
    a1 = acc_ref[1]
    o = a0[0:DA_V_DIM] / a0[DA_V_DIM:DA_V_DIM + 1] - lam * (a1[0:DA_V_DIM] / a1[DA_V_DIM:DA_V_DIM + 1])
    o_ref[0] = o.T


def _attn_online_body(qT_ref, kaug_ref, vT_ref, lamv_ref, shift_ref, o_ref, qw_ref, acc_ref, d_ref, m_ref,
                      *, T, nk, lambda_init):
    del shift_ref
    qi = pl.program_id(2)
    _attn_setup(qT_ref, jnp.zeros((1, 1), _f32), qw_ref, acc_ref, d_ref, T=T)
    m_ref[...] = jnp.full(m_ref.shape, -1e30, _f32)

    def step(kt, side, diag):
        ktile = kaug_ref[0, 0, pl.ds(pl.multiple_of(kt * T, T), T), :]
        vtile = vT_ref[0, 0, kt]
        for mp in range(2):
            s = jnp.dot(ktile, qw_ref[2 * side + mp], preferred_element_type=_f32)
            if diag:
                s = s + d_ref[...]
            m_old = m_ref[mp]
            m_new = jnp.maximum(m_old, jnp.max(s, axis=0, keepdims=True))
            alpha = jnp.exp(m_old - m_new)
            p = jnp.exp(s - m_new).astype(_bf16)
            pv = jnp.dot(vtile, p, preferred_element_type=_f32)
            acc_ref[mp] = acc_ref[mp] * alpha + pv
            m_ref[mp] = m_new

    def before(kt, carry):
        step(kt, 0, False)
        return carry

    def after(kt, carry):
        step(kt, 1, False)
        return carry

    lax.fori_loop(0, qi, before, 0)
    step(qi, 0, True)
    lax.fori_loop(qi + 1, nk, after, 0)
    _attn_finish(lamv_ref, acc_ref, o_ref, lambda_init=lambda_init)


def _attn_shifted_body(qT_ref, kaug_ref, vT_ref, lamv_ref, shift_ref, o_ref, qw_ref, acc_ref, d_ref, s_ref,
                       *, T, nk, lambda_init):
    qi = pl.program_id(2)
    _attn_setup(qT_ref, shift_ref[...], qw_ref, acc_ref, d_ref, T=T)

    def scores(kt, slot):
        ktile = kaug_ref[0, 0, pl.ds(pl.multiple_of(kt * T, T), T), :]
        side = (kt > qi).astype(jnp.int32)
        for mp in range(2):
            s_ref[slot, mp] = jnp.dot(ktile, qw_ref[2 * side + mp], preferred_element_type=_f32)

    def fix_diagonal(kt, slot):
        @pl.when(kt == qi)
        def _():
            for mp in range(2):
                s_ref[slot, mp] += d_ref[...]

    def consume(kt, slot):
        vtile = vT_ref[0, 0, kt]
        for mp in range(2):
            p = jnp.exp(s_ref[slot, mp]).astype(_bf16)
            acc_ref[mp] += jnp.dot(vtile, p, preferred_element_type=_f32)

    def pair(u, last):
        t0 = 2 * u
        fix_diagonal(t0, 0)
        scores(t0 + 1, 1)
        consume(t0, 0)
        fix_diagonal(t0 + 1, 1)
        if not last:
            scores(t0 + 2, 0)
        consume(t0 + 1, 1)

    def body(u, carry):
        pair(u, False)
        return carry

    scores(0, 0)
    lax.fori_loop(0, nk // 2 - 1, body, 0)
    pair(nk // 2 - 1, True)
    _attn_finish(lamv_ref, acc_ref, o_ref, lambda_init=lambda_init)


def _attn(qT, kaug, vT, lamv, shift, *, batch, seq, T, lambda_init, shifted):
    nk = seq // T
    assert nk % 2 == 0
    if shifted:
        body = _attn_shifted_body
        extra = pltpu.VMEM((2, 2, T, T), _f32)
    else:
        body = _attn_online_body
        extra = pltpu.VMEM((2, 1, T), _f32)
    return pl.pallas_call(
        functools.partial(body, T=T, nk=nk, lambda_init=lambda_init),
        grid=(batch, DA_HEADS, nk),
        in_specs=[
            pl.BlockSpec((1, 1, 128, T), lambda b, h, q: (b, h, 0, q)),
            pl.BlockSpec((1, 1, seq, _KAUG_LANES), lambda b, h, q: (b, h, 0, 0)),
            pl.BlockSpec((1, 1, nk, _VT_ROWS, T), lambda b, h, q: (b, h, 0, 0, 0)),
            pl.BlockSpec((4, DA_QK_DIM), lambda b, h, q: (0, 0)),
            pl.BlockSpec((1, 1), lambda b, h, q: (0, 0)),
        ],
        out_specs=pl.BlockSpec((1, T, DA_V_DIM), lambda b, h, q: (b, q, h)),
        out_shape=jax.ShapeDtypeStruct((batch, seq, DA_WIDTH), _f32),
        scratch_shapes=[
            pltpu.VMEM((4, _KAUG_LANES, T), _bf16),
            pltpu.VMEM((2, _VT_ROWS, T), _f32),
            pltpu.VMEM((T, T), _f32),
            extra,
        ],
        compiler_params=pltpu.CompilerParams(
            dimension_semantics=("arbitrary", "arbitrary", "arbitrary"),
            vmem_limit_bytes=_VMEM_LIMIT_BYTES),
        name="diffattn_shifted" if shifted else "diffattn_online",
    )(qT, kaug, vT, lamv, shift)


def _log_sigmoid(x):
    return jnp.minimum(x, 0.0) - jnp.log1p(jnp.exp(-jnp.abs(x)))


def _ret_body(q_ref, k_ref, v_ref, dec_ref, o_ref, dm_ref, kvf_ref, kvb_ref, sf_ref, sb_ref, *, C, nc):
    hd = pl.program_id(1)
    lane = lax.broadcasted_iota(jnp.int32, (1, 128), 1)
    hmask = (lane // RET_QK_DIM) == (hd % 2)

    lg = _log_sigmoid(dec_ref[...])
    sel = lane == hd
    lgf = jnp.sum(jnp.where(sel, lg[0:1], 0.0), axis=-1, keepdims=True)
    lgb = jnp.sum(jnp.where(sel, lg[1:2], 0.0), axis=-1, keepdims=True)

    n_col = lax.broadcasted_iota(jnp.int32, (C, 1), 0).astype(_f32)
    tt = lax.broadcasted_iota(jnp.int32, (C, C), 0)
    ss = lax.broadcasted_iota(jnp.int32, (C, C), 1)
    diff = (tt - ss).astype(_f32)
    dm_ref[...] = jnp.where(diff >= 0.0, jnp.exp(lgf * jnp.maximum(diff, 0.0)),
                            jnp.exp(lgb * jnp.maximum(-diff, 0.0)))
    zeta_f = jnp.exp(lgf * (C - 1.0 - n_col))
    xi_f = jnp.exp(lgf * (n_col + 1.0))
    zeta_b = jnp.exp(lgb * (n_col + 1.0))
    xi_b = jnp.exp(lgb * (C - 1.0 - n_col))
    decay_f = jnp.exp(lgf * float(C))
    decay_b = jnp.exp(lgb * float(C))

    def rows_of(c):
        return pl.ds(pl.multiple_of(c * C, C), C)

    def intra(c, carry):
        rows = rows_of(c)
        q = jnp.where(hmask, q_ref[0, rows, :], jnp.zeros((), _bf16))
        k = k_ref[0, rows, :]
        v = v_ref[0, rows, :]
        a = lax.dot_general(q, k, (((1,), (1,)), ((), ())), preferred_element_type=_f32)
        a = (a * dm_ref[...]).astype(_bf16)
        o_ref[0, rows, :] = jnp.dot(a, v, preferred_element_type=_f32)
        kf = (k.astype(_f32) * zeta_f).astype(_bf16)
        kb = (k.astype(_f32) * zeta_b).astype(_bf16)
        kvf_ref[c] = lax.dot_general(kf, v, (((0,), (0,)), ((), ())), preferred_element_type=_f32)
        kvb_ref[c] = lax.dot_general(kb, v, (((0,), (0,)), ((), ())), preferred_element_type=_f32)
        return carry

    lax.fori_loop(0, nc, intra, 0)

    def scan_f(c, state):
        sf_ref[c] = state.astype(_bf16)
        return decay_f * state + kvf_ref[c]

    def scan_b(i, state):
        c = nc - 1 - i
        sb_ref[c] = state.astype(_bf16)
        return decay_b * state + kvb_ref[c]

    zero_state = jnp.zeros((128, RET_V_DIM), _f32)
    lax.fori_loop(0, nc, scan_f, zero_state)
    lax.fori_loop(0, nc, scan_b, zero_state)

    def cross(c, carry):
        rows = rows_of(c)
        q = jnp.where(hmask, q_ref[0, rows, :], jnp.zeros((), _bf16)).astype(_f32)
        qf = (q * xi_f).astype(_bf16)
        qb = (q * xi_b).astype(_bf16)
        o_ref[0, rows, :] += (jnp.dot(qf, sf_ref[c], preferred_element_type=_f32)
                              + jnp.dot(qb, sb_ref[c], preferred_element_type=_f32))
        return carry

    lax.fori_loop(0, nc, cross, 0)


def _retention(qr, kr, vr, dec, *, batch, seq, C):
    nc = seq // C
    return pl.pallas_call(
        functools.partial(_ret_body, C=C, nc=nc),
        grid=(batch, RET_HEADS),
        in_specs=[
            pl.BlockSpec((1, seq, 128), lambda b, h: (b, 0, h // 2)),
            pl.BlockSpec((1, seq, 128), lambda b, h: (b, 0, h // 2)),
            pl.BlockSpec((1, seq, RET_V_DIM), lambda b, h: (b, 0, h)),
            pl.BlockSpec((2, 128), lambda b, h: (0, 0)),
        ],
        out_specs=pl.BlockSpec((1, seq, RET_V_DIM), lambda b, h: (b, 0, h)),
        out_shape=jax.ShapeDtypeStruct((batch, seq, RET_WIDTH), _f32),
        scratch_shapes=[
            pltpu.VMEM((C, C), _f32),
            pltpu.VMEM((nc, 128, RET_V_DIM), _f32),
            pltpu.VMEM((nc, 128, RET_V_DIM), _f32),
            pltpu.VMEM((nc, 128, RET_V_DIM), _bf16),
            pltpu.VMEM((nc, 128, RET_V_DIM), _bf16),
        ],
        compiler_params=pltpu.CompilerParams(
            dimension_semantics=("arbitrary", "arbitrary"), vmem_limit_bytes=_VMEM_LIMIT_BYTES),
        name="retention",
    )(qr, kr, vr, dec)


def _outproj_body(x_ref, oa_ref, or_ref, gate_ref, sw_ref, w_ref, o_ref, *, lambda_init):
    oa = oa_ref[...]
    orr = or_ref[...]
    pieces = []
    for hd in range(DA_HEADS):
        y = oa[:, 128 * hd:128 * hd + 128]
        y = y * lax.rsqrt(jnp.mean(y * y, axis=-1, keepdims=True) + EPS) * sw_ref[...]
        pieces.append(y * (1.0 - lambda_init))
    for hd in range(RET_HEADS):
        y = orr[:, 128 * hd:128 * hd + 128]
        pieces.append(y * lax.rsqrt(jnp.mean(y * y, axis=-1, keepdims=True) + EPS))
    mixed = jnp.concatenate(pieces, axis=-1) * gate_ref[...].astype(_f32)
    o_ref[...] = x_ref[...] + jnp.dot(mixed.astype(_bf16), w_ref[...], preferred_element_type=_f32)


def _outproj(x2d, oa2d, or2d, gates, subln_w, w_out_bf16, *, tm, lambda_init):
    rows = x2d.shape[0]
    return pl.pallas_call(
        functools.partial(_outproj_body, lambda_init=lambda_init),
        grid=(rows // tm,),
        in_specs=[
            pl.BlockSpec((tm, D_MODEL), lambda i: (i, 0)),
            pl.BlockSpec((tm, DA_WIDTH), lambda i: (i, 0)),
            pl.BlockSpec((tm, RET_WIDTH), lambda i: (i, 0)),
            pl.BlockSpec((tm, DA_WIDTH + RET_WIDTH), lambda i: (i, 0)),
            pl.BlockSpec((1, DA_V_DIM), lambda i: (0, 0)),
            pl.BlockSpec((DA_WIDTH + RET_WIDTH, D_MODEL), lambda i: (0, 0)),
        ],
        out_specs=pl.BlockSpec((tm, D_MODEL), lambda i: (i, 0)),
        out_shape=jax.ShapeDtypeStruct((rows, D_MODEL), _f32),
        compiler_params=pltpu.CompilerParams(
            dimension_semantics=("arbitrary",), vmem_limit_bytes=_VMEM_LIMIT_BYTES),
        name="outproj",
    )(x2d, oa2d, or2d, gates, subln_w, w_out_bf16)


def _tiles(seq):
    attn_tile = min(512, seq // 2)
    ret_chunk = min(256, seq // 2)
    return attn_tile, ret_chunk


def kernel(x, norm_w, w_in, q_norm_w, k_norm_w, lambda_q1, lambda_k1, lambda_q2, lambda_k2,
           subln_w, ret_decay_fwd, ret_decay_bwd, w_out):
    batch, seq, _ = x.shape
    depth = norm_w.shape[0]
    T, C = _tiles(seq)
    rows = batch * seq

    lane = jnp.arange(128)
    gsum = (lane[:, None] // DA_QK_DIM == lane[None, :] // DA_QK_DIM).astype(_bf16)

    x2d = x.reshape(rows, D_MODEL)
    for l in range(depth):
        lambda_init = 0.8 - 0.6 * math.exp(-0.3 * l)
        qnw2 = jnp.tile(q_norm_w[l], 2).reshape(1, 128)
        knw2 = jnp.tile(k_norm_w[l], 2).reshape(1, 128)
        qT, kaug, vT, gates, qr, kr, vr = _inproj(
            x2d, norm_w[l].reshape(1, D_MODEL), w_in[l].astype(_bf16), qnw2, knw2, gsum,
            batch=batch, seq=seq, tm=T)
        lamv = jnp.stack([lambda_q1[l], lambda_k1[l], lambda_q2[l], lambda_k2[l]]).astype(_f32)
        bound = (_BOUND_SLACK * math.sqrt(DA_QK_DIM) * jnp.max(jnp.abs(q_norm_w[l]))
                 * jnp.max(jnp.abs(k_norm_w[l]))).astype(_f32)
        attn = functools.partial(_attn, batch=batch, seq=seq, T=T, lambda_init=lambda_init)
        oa = lax.cond(bound <= _MAX_FIXED_SHIFT,
                      functools.partial(attn, shifted=True), functools.partial(attn, shifted=False),
                      qT, kaug, vT, lamv, (-bound).reshape(1, 1))
        dec = jnp.zeros((2, 128), _f32)
        dec = dec.at[0, :RET_HEADS].set(ret_decay_fwd[l]).at[1, :RET_HEADS].set(ret_decay_bwd[l])
        orr = _retention(qr.reshape(batch, seq, 256), kr.reshape(batch, seq, 256),
                         vr.reshape(batch, seq, RET_WIDTH), dec, batch=batch, seq=seq, C=C)
        x2d = _outproj(x2d, oa.reshape(rows, DA_WIDTH), orr.reshape(rows, RET_WIDTH), gates,
                       subln_w[l].reshape(1, DA_V_DIM), w_out[l].astype(_bf16),
                       tm=T, lambda_init=lambda_init)
    return x2d.reshape(batch, seq, D_MODEL)
```

```python
import functools
import math

import jax
import jax.numpy as jnp
from jax import lax
from jax.experimental import pallas as pl
from jax.experimental.pallas import tpu as pltpu

D_MODEL = 1024
DA_HEADS = 4
DA_QK_DIM = 64
DA_V_DIM = 128
DA_WIDTH = DA_HEADS * DA_V_DIM
RET_HEADS = 4
RET_QK_DIM = 64
RET_V_DIM = 128
RET_WIDTH = RET_HEADS * RET_V_DIM
IN_WIDTH = 3584
EPS = 1e-6

_QA, _KA, _VA, _GA, _QR, _KR, _VR, _GR = 0, 512, 1024, 1536, 2048, 2304, 2560, 3072

_POS_SPLIT = 256
_KAUG_LANES = 256
_VT_ROWS = 144
_VMEM_LIMIT_BYTES = 56 * 1024 * 1024
_ROW_SPLITS = 2
_MAX_FIXED_SHIFT = 40.0
_BOUND_SLACK = 1.02

_f32 = jnp.float32
_bf16 = jnp.bfloat16


def _alibi_slope(head):
    return 2.0 ** (-8.0 * (head + 1) / DA_HEADS)


def _inproj_body(x_ref, nw_ref, w_ref, qnw_ref, knw_ref, gsum_ref,
                 qT_ref, kaug_ref, vT_ref, gate_ref, qr_ref, kr_ref, vr_ref, *, tm, tiles_per_seq):
    gsum = gsum_ref[...]

    def group_rms(y, w):
        y2 = y * y
        hi = y2.astype(_bf16)
        lo = (y2 - hi.astype(_f32)).astype(_bf16)
        ss = (jnp.dot(hi, gsum, preferred_element_type=_f32)
              + jnp.dot(lo, gsum, preferred_element_type=_f32))
        return y * lax.rsqrt(ss * (1.0 / DA_QK_DIM) + EPS) * w

    st = pl.program_id(0) % tiles_per_seq
    hm = tm // _ROW_SPLITS
    scale = DA_QK_DIM ** -0.5
    vrow = lax.broadcasted_iota(jnp.int32, (_VT_ROWS - DA_V_DIM, hm), 0)
    ones_rows = jnp.where(vrow == 0, 1.0, 0.0).astype(_bf16)
    lane = lax.broadcasted_iota(jnp.int32, (hm, 128), 1)

    for r in range(_ROW_SPLITS):
        rs = slice(r * hm, (r + 1) * hm)
        x = x_ref[rs, :]
        ms = jnp.mean(x * x, axis=-1, keepdims=True)
        xn = (x * lax.rsqrt(ms + EPS) * nw_ref[...]).astype(_bf16)
        h = jnp.dot(xn, w_ref[...], preferred_element_type=_f32)

        row = lax.broadcasted_iota(jnp.int32, (hm, 128), 0) + (st * tm + r * hm)
        pos_lo = (row % _POS_SPLIT).astype(_f32)
        pos_hi = ((row // _POS_SPLIT) * _POS_SPLIT).astype(_f32)

        for hd in range(DA_HEADS):
            c = 128 * hd
            q = group_rms(h[:, _QA + c:_QA + c + 128], qnw_ref[...]) * scale
            qT_ref[0, hd, :, rs] = q.T.astype(_bf16)
            k = group_rms(h[:, _KA + c:_KA + c + 128], knw_ref[...])
            kaug_ref[0, hd, rs, 0:128] = k.astype(_bf16)
            slope = _alibi_slope(hd)
            kpos = jnp.where(lane == 0, slope * pos_lo,
                             jnp.where(lane == 1, slope * pos_hi,
                                       jnp.where(lane < 5, 1.0, 0.0)))
            kaug_ref[0, hd, rs, 128:256] = kpos.astype(_bf16)
            v = h[:, _VA + c:_VA + c + 128]
            vT_ref[0, hd, 0, 0:DA_V_DIM, rs] = v.T.astype(_bf16)
            vT_ref[0, hd, 0, DA_V_DIM:_VT_ROWS, rs] = ones_rows

        ga = h[:, _GA:_GA + DA_WIDTH]
        gr = h[:, _GR:_GR + RET_WIDTH]
        gate_ref[rs, 0:DA_WIDTH] = (ga * jax.nn.sigmoid(ga)).astype(_bf16)
        gate_ref[rs, DA_WIDTH:DA_WIDTH + RET_WIDTH] = (gr * jax.nn.sigmoid(gr)).astype(_bf16)
        qr_ref[rs, :] = h[:, _QR:_QR + 256].astype(_bf16)
        kr_ref[rs, :] = (h[:, _KR:_KR + 256] * (RET_QK_DIM ** -0.5)).astype(_bf16)
        vr_ref[rs, :] = h[:, _VR:_VR + RET_WIDTH].astype(_bf16)


def _inproj(x2d, norm_w, w_in_bf16, qnw2, knw2, gsum, *, batch, seq, tm):
    rows = batch * seq
    tiles_per_seq = seq // tm
    nk = seq // tm
    full = lambda shape: pl.BlockSpec(shape, lambda i: (0,) * len(shape))
    b_of = lambda i: i // tiles_per_seq
    s_of = lambda i: i % tiles_per_seq
    out_shape = (
        jax.ShapeDtypeStruct((batch, DA_HEADS, 128, seq), _bf16),
        jax.ShapeDtypeStruct((batch, DA_HEADS, seq, _KAUG_LANES), _bf16),
        jax.ShapeDtypeStruct((batch, DA_HEADS, nk, _VT_ROWS, tm), _bf16),
        jax.ShapeDtypeStruct((rows, DA_WIDTH + RET_WIDTH), _bf16),
        jax.ShapeDtypeStruct((rows, 256), _bf16),
        jax.ShapeDtypeStruct((rows, 256), _bf16),
        jax.ShapeDtypeStruct((rows, RET_WIDTH), _bf16),
    )
    out_specs = (
        pl.BlockSpec((1, DA_HEADS, 128, tm), lambda i: (b_of(i), 0, 0, s_of(i))),
        pl.BlockSpec((1, DA_HEADS, tm, _KAUG_LANES), lambda i: (b_of(i), 0, s_of(i), 0)),
        pl.BlockSpec((1, DA_HEADS, 1, _VT_ROWS, tm), lambda i: (b_of(i), 0, s_of(i), 0, 0)),
        pl.BlockSpec((tm, DA_WIDTH + RET_WIDTH), lambda i: (i, 0)),
        pl.BlockSpec((tm, 256), lambda i: (i, 0)),
        pl.BlockSpec((tm, 256), lambda i: (i, 0)),
        pl.BlockSpec((tm, RET_WIDTH), lambda i: (i, 0)),
    )
    return pl.pallas_call(
        functools.partial(_inproj_body, tm=tm, tiles_per_seq=tiles_per_seq),
        grid=(rows // tm,),
        in_specs=[
            pl.BlockSpec((tm, D_MODEL), lambda i: (i, 0)),
            full((1, D_MODEL)),
            full((D_MODEL, IN_WIDTH)),
            full((1, 128)),
            full((1, 128)),
            full((128, 128)),
        ],
        out_specs=out_specs,
        out_shape=out_shape,
        compiler_params=pltpu.CompilerParams(
            dimension_semantics=("arbitrary",), vmem_limit_bytes=_VMEM_LIMIT_BYTES),
        name="inproj",
    )(x2d, norm_w, w_in_bf16, qnw2, knw2, gsum)


def _attn_setup(qT_ref, shift, qw_ref, acc_ref, d_ref, *, T):
    hd = pl.program_id(1)
    qi = pl.program_id(2)
    slope = jnp.where(hd == 0, _alibi_slope(0),
                      jnp.where(hd == 1, _alibi_slope(1),
                                jnp.where(hd == 2, _alibi_slope(2), _alibi_slope(3)))).astype(_f32)

    @pl.when(qi == 0)
    def _():
        jj = lax.broadcasted_iota(jnp.int32, (T, T), 0)
        ii = lax.broadcasted_iota(jnp.int32, (T, T), 1)
        d_ref[...] = (-2.0 * slope) * jnp.maximum(jj - ii, 0).astype(_f32)

    qt = qT_ref[0, 0]
    prow = lax.broadcasted_iota(jnp.int32, (16, T), 0)
    ipos = lax.broadcasted_iota(jnp.int32, (16, T), 1) + qi * T
    ilo = (ipos % _POS_SPLIT).astype(_f32)
    ihi = ((ipos // _POS_SPLIT) * _POS_SPLIT).astype(_f32)
    qpos = jnp.where(prow < 2, 1.0,
                     jnp.where(prow == 2, -slope * ilo,
                               jnp.where(prow == 3, -slope * ihi, 0.0)))
    shift_row = jnp.where(prow == 4, shift, 0.0)
    zeros64 = jnp.zeros((64, T), _bf16)
    for side in range(2):
        aug = ((qpos if side == 0 else -qpos) + shift_row).astype(_bf16)
        for mp in range(2):
            idx = 2 * side + mp
            qw_ref[idx, 0:64, :] = qt[0:64] if mp == 0 else zeros64
            qw_ref[idx, 64:128, :] = zeros64 if mp == 0 else qt[64:128]
            qw_ref[idx, 128:144, :] = aug
            qw_ref[idx, 144:_KAUG_LANES, :] = jnp.zeros((_KAUG_LANES - 144, T), _bf16)

    acc_ref[...] = jnp.zeros_like(acc_ref)


def _attn_finish(lamv_ref, acc_ref, o_ref, *, lambda_init):
    lv = lamv_ref[...]
    lam = (jnp.exp(jnp.sum(lv[0:1] * lv[1:2], axis=-1, keepdims=True))
           - jnp.exp(jnp.sum(lv[2:3] * lv[3:4], axis=-1, keepdims=True)) + lambda_init)
    a0 = acc_ref[0]
    a1 = acc_ref[1]
    o = a0[0:DA_V_DIM] / a0[DA_V_DIM:DA_V_DIM + 1] - lam * (a1[0:DA_V_DIM] / a1[DA_V_DIM:DA_V_DIM + 1])
    o_ref[0] = o.T


def _attn_online_body(qT_ref, kaug_ref, vT_ref, lamv_ref, shift_ref, o_ref, qw_ref, acc_ref, d_ref, m_ref,
                      *, T, nk, lambda_init):
    del shift_ref
    qi = pl.program_id(2)
    _attn_setup(qT_ref, jnp.zeros((1, 1), _f32), qw_ref, acc_ref, d_ref, T=T)
    m_ref[...] = jnp.full(m_ref.shape, -1e30, _f32)

    def step(kt, side, diag):
        ktile = kaug_ref[0, 0, pl.ds(pl.multiple_of(kt * T, T), T), :]
        vtile = vT_ref[0, 0, kt]
        for mp in range(2):
            s = jnp.dot(ktile, qw_ref[2 * side + mp], preferred_element_type=_f32)
            if diag:
                s = s + d_ref[...]
            m_old = m_ref[mp]
            m_new = jnp.maximum(m_old, jnp.max(s, axis=0, keepdims=True))
            alpha = jnp.exp(m_old - m_new)
            p = jnp.exp(s - m_new).astype(_bf16)
            pv = jnp.dot(vtile, p, preferred_element_type=_f32)
            acc_ref[mp] = acc_ref[mp] * alpha + pv
            m_ref[mp] = m_new

    def before(kt, carry):
        step(kt, 0, False)
        return carry

    def after(kt, carry):
        step(kt, 1, False)
        return carry

    lax.fori_loop(0, qi, before, 0)
    step(qi, 0, True)
    lax.fori_loop(qi + 1, nk, after, 0)
    _attn_finish(lamv_ref, acc_ref, o_ref, lambda_init=lambda_init)


def _attn_shifted_body(qT_ref, kaug_ref, vT_ref, lamv_ref, shift_ref, o_ref, qw_ref, acc_ref, d_ref, s_ref,
                       *, T, nk, U, lambda_init):
    qi = pl.program_id(2)
    _attn_setup(qT_ref, shift_ref[...], qw_ref, acc_ref, d_ref, T=T)

    def key_tile(j):
        return j + (j >= qi).astype(jnp.int32)

    def scores(j, slot):
        kt = key_tile(j)
        ktile = kaug_ref[0, 0, pl.ds(pl.multiple_of(kt * T, T), T), :]
        side = (kt > qi).astype(jnp.int32)
        for mp in range(2):
            s_ref[slot, mp] = jnp.dot(ktile, qw_ref[2 * side + mp], preferred_element_type=_f32)

    def consume(j, slot):
        vtile = vT_ref[0, 0, key_tile(j)]
        for mp in range(2):
            p = jnp.exp(s_ref[slot, mp]).astype(_bf16)
            acc_ref[mp] += jnp.dot(vtile, p, preferred_element_type=_f32)

    def group(g, last):
        for a in range(U):
            if not (last and a == U - 1):
                scores(g * U + a + 1, (a + 1) % U % 3)
            consume(g * U + a, a % 3)

    def body(g, carry):
        group(g, False)
        return carry

    scores(0, 0)
    kdiag = kaug_ref[0, 0, pl.ds(pl.multiple_of(qi * T, T), T), :]
    vdiag = vT_ref[0, 0, qi]
    for mp in range(2):
        s = jnp.dot(kdiag, qw_ref[mp], preferred_element_type=_f32) + d_ref[...]
        acc_ref[mp] = jnp.dot(vdiag, jnp.exp(s).astype(_bf16), preferred_element_type=_f32)

    n_groups = (nk - 1) // U
    lax.fori_loop(0, n_groups - 1, body, 0)
    group(n_groups - 1, True)
    _attn_finish(lamv_ref, acc_ref, o_ref, lambda_init=lambda_init)


def _attn(qT, kaug, vT, lamv, shift, *, batch, seq, T, lambda_init, shifted):
    nk = seq // T
    if shifted:
        tiles_per_group = next(u for u in (5, 3, 2) if (nk - 1) % u == 0) if nk > 2 else 1
        body = functools.partial(_attn_shifted_body, U=tiles_per_group)
        extra = pltpu.VMEM((3, 2, T, T), _f32)
    else:
        body = _attn_online_body
        extra = pltpu.VMEM((2, 1, T), _f32)
    return pl.pallas_call(
        functools.partial(body, T=T, nk=nk, lambda_init=lambda_init),
        grid=(batch, DA_HEADS, nk),
        in_specs=[
            pl.BlockSpec((1, 1, 128, T), lambda b, h, q: (b, h, 0, q)),
            pl.BlockSpec((1, 1, seq, _KAUG_LANES), lambda b, h, q: (b, h, 0, 0)),
            pl.BlockSpec((1, 1, nk, _VT_ROWS, T), lambda b, h, q: (b, h, 0, 0, 0)),
            pl.BlockSpec((4, DA_QK_DIM), lambda b, h, q: (0, 0)),
            pl.BlockSpec((1, 1), lambda b, h, q: (0, 0)),
        ],
        out_specs=pl.BlockSpec((1, T, DA_V_DIM), lambda b, h, q: (b, q, h)),
        out_shape=jax.ShapeDtypeStruct((batch, seq, DA_WIDTH), _f32),
        scratch_shapes=[
            pltpu.VMEM((4, _KAUG_LANES, T), _bf16),
            pltpu.VMEM((2, _VT_ROWS, T), _f32),
            pltpu.VMEM((T, T), _f32),
            extra,
        ],
        compiler_params=pltpu.CompilerParams(
            dimension_semantics=("arbitrary", "arbitrary", "arbitrary"),
            vmem_limit_bytes=_VMEM_LIMIT_BYTES),
        name="diffattn_shifted" if shifted else "diffattn_online",
    )(qT, kaug, vT, lamv, shift)


def _log_sigmoid(x):
    return jnp.minimum(x, 0.0) - jnp.log1p(jnp.exp(-jnp.abs(x)))


def _ret_body(q_ref, k_ref, v_ref, dec_ref, o_ref, dm_ref, kvf_ref, kvb_ref, sf_ref, sb_ref, *, C, nc):
    hd = pl.program_id(1)
    lane = lax.broadcasted_iota(jnp.int32, (1, 128), 1)
    hmask = (lane // RET_QK_DIM) == (hd % 2)

    lg = _log_sigmoid(dec_ref[...])
    sel = lane == hd
    lgf = jnp.sum(jnp.where(sel, lg[0:1], 0.0), axis=-1, keepdims=True)
    lgb = jnp.sum(jnp.where(sel, lg[1:2], 0.0), axis=-1, keepdims=True)

    n_col = lax.broadcasted_iota(jnp.int32, (C, 1), 0).astype(_f32)
    tt = lax.broadcasted_iota(jnp.int32, (C, C), 0)
    ss = lax.broadcasted_iota(jnp.int32, (C, C), 1)
    diff = (tt - ss).astype(_f32)
    dm_ref[...] = jnp.where(diff >= 0.0, jnp.exp(lgf * jnp.maximum(diff, 0.0)),
                            jnp.exp(lgb * jnp.maximum(-diff, 0.0)))
    zeta_f = jnp.exp(lgf * (C - 1.0 - n_col))
    xi_f = jnp.exp(lgf * (n_col + 1.0))
    zeta_b = jnp.exp(lgb * (n_col + 1.0))
    xi_b = jnp.exp(lgb * (C - 1.0 - n_col))
    decay_f = jnp.exp(lgf * float(C))
    decay_b = jnp.exp(lgb * float(C))

    def rows_of(c):
        return pl.ds(pl.multiple_of(c * C, C), C)

    def intra(c, carry):
        rows = rows_of(c)
        q = jnp.where(hmask, q_ref[0, rows, :], jnp.zeros((), _bf16))
        k = k_ref[0, rows, :]
        v = v_ref[0, rows, :]
        a = lax.dot_general(q, k, (((1,), (1,)), ((), ())), preferred_element_type=_f32)
        a = (a * dm_ref[...]).astype(_bf16)
        o_ref[0, rows, :] = jnp.dot(a, v, preferred_element_type=_f32)
        kf = (k.astype(_f32) * zeta_f).astype(_bf16)
        kb = (k.astype(_f32) * zeta_b).astype(_bf16)
        kvf_ref[c] = lax.dot_general(kf, v, (((0,), (0,)), ((), ())), preferred_element_type=_f32)
        kvb_ref[c] = lax.dot_general(kb, v, (((0,), (0,)), ((), ())), preferred_element_type=_f32)
        return carry

    lax.fori_loop(0, nc, intra, 0, unroll=4)

    def scan_f(c, state):
        sf_ref[c] = state.astype(_bf16)
        return decay_f * state + kvf_ref[c]

    def scan_b(i, state):
        c = nc - 1 - i
        sb_ref[c] = state.astype(_bf16)
        return decay_b * state + kvb_ref[c]

    zero_state = jnp.zeros((128, RET_V_DIM), _f32)
    lax.fori_loop(0, nc, scan_f, zero_state)
    lax.fori_loop(0, nc, scan_b, zero_state)

    def cross(c, carry):
        rows = rows_of(c)
        q = jnp.where(hmask, q_ref[0, rows, :], jnp.zeros((), _bf16)).astype(_f32)
        qf = (q * xi_f).astype(_bf16)
        qb = (q * xi_b).astype(_bf16)
        o_ref[0, rows, :] += (jnp.dot(qf, sf_ref[c], preferred_element_type=_f32)
                              + jnp.dot(qb, sb_ref[c], preferred_element_type=_f32))
        return carry

    lax.fori_loop(0, nc, cross, 0, unroll=4)


def _retention(qr, kr, vr, dec, *, batch, seq, C):
    nc = seq // C
    return pl.pallas_call(
        functools.partial(_ret_body, C=C, nc=nc),
        grid=(batch, RET_HEADS),
        in_specs=[
            pl.BlockSpec((1, seq, 128), lambda b, h: (b, 0, h // 2)),
            pl.BlockSpec((1, seq, 128), lambda b, h: (b, 0, h // 2)),
            pl.BlockSpec((1, seq, RET_V_DIM), lambda b, h: (b, 0, h)),
            pl.BlockSpec((2, 128), lambda b, h: (0, 0)),
        ],
        out_specs=pl.BlockSpec((1, seq, RET_V_DIM), lambda b, h: (b, 0, h)),
        out_shape=jax.ShapeDtypeStruct((batch, seq, RET_WIDTH), _f32),
        scratch_shapes=[
            pltpu.VMEM((C, C), _f32),
            pltpu.VMEM((nc, 128, RET_V_DIM), _f32),
            pltpu.VMEM((nc, 128, RET_V_DIM), _f32),
            pltpu.VMEM((nc, 128, RET_V_DIM), _bf16),
            pltpu.VMEM((nc, 128, RET_V_DIM), _bf16),
        ],
        compiler_params=pltpu.CompilerParams(
            dimension_semantics=("arbitrary", "arbitrary"), vmem_limit_bytes=_VMEM_LIMIT_BYTES),
        name="retention",
    )(qr, kr, vr, dec)


def _outproj_body(x_ref, oa_ref, or_ref, gate_ref, sw_ref, w_ref, o_ref, *, lambda_init):
    oa = oa_ref[...]
    orr = or_ref[...]
    pieces = []
    for hd in range(DA_HEADS):
        y = oa[:, 128 * hd:128 * hd + 128]
        y = y * lax.rsqrt(jnp.mean(y * y, axis=-1, keepdims=True) + EPS) * sw_ref[...]
        pieces.append(y * (1.0 - lambda_init))
    for hd in range(RET_HEADS):
        y = orr[:, 128 * hd:128 * hd + 128]
        pieces.append(y * lax.rsqrt(jnp.mean(y * y, axis=-1, keepdims=True) + EPS))
    mixed = jnp.concatenate(pieces, axis=-1) * gate_ref[...].astype(_f32)
    o_ref[...] = x_ref[...] + jnp.dot(mixed.astype(_bf16), w_ref[...], preferred_element_type=_f32)


def _outproj(x2d, oa2d, or2d, gates, subln_w, w_out_bf16, *, tm, lambda_init):
    rows = x2d.shape[0]
    return pl.pallas_call(
        functools.partial(_outproj_body, lambda_init=lambda_init),
        grid=(rows // tm,),
        in_specs=[
            pl.BlockSpec((tm, D_MODEL), lambda i: (i, 0)),
            pl.BlockSpec((tm, DA_WIDTH), lambda i: (i, 0)),
            pl.BlockSpec((tm, RET_WIDTH), lambda i: (i, 0)),
            pl.BlockSpec((tm, DA_WIDTH + RET_WIDTH), lambda i: (i, 0)),
            pl.BlockSpec((1, DA_V_DIM), lambda i: (0, 0)),
            pl.BlockSpec((DA_WIDTH + RET_WIDTH, D_MODEL), lambda i: (0, 0)),
        ],
        out_specs=pl.BlockSpec((tm, D_MODEL), lambda i: (i, 0)),
        out_shape=jax.ShapeDtypeStruct((rows, D_MODEL), _f32),
        compiler_params=pltpu.CompilerParams(
            dimension_semantics=("arbitrary",), vmem_limit_bytes=_VMEM_LIMIT_BYTES),
        name="outproj",
    )(x2d, oa2d, or2d, gates, subln_w, w_out_bf16)


def _tiles(seq):
    attn_tile = min(512, seq // 2)
    ret_chunk = min(256, seq // 2)
    return attn_tile, ret_chunk


def kernel(x, norm_w, w_in, q_norm_w, k_norm_w, lambda_q1, lambda_k1, lambda_q2, lambda_k2,
           subln_w, ret_decay_fwd, ret_decay_bwd, w_out):
    batch, seq, _ = x.shape
    depth = norm_w.shape[0]
    T, C = _tiles(seq)
    rows = batch * seq

    lane = jnp.arange(128)
    gsum = (lane[:, None] // DA_QK_DIM == lane[None, :] // DA_QK_DIM).astype(_bf16)

    x2d = x.reshape(rows, D_MODEL)
    for l in range(depth):
        lambda_init = 0.8 - 0.6 * math.exp(-0.3 * l)
        qnw2 = jnp.tile(q_norm_w[l], 2).reshape(1, 128)
        knw2 = jnp.tile(k_norm_w[l], 2).reshape(1, 128)
        qT, kaug, vT, gates, qr, kr, vr = _inproj(
            x2d, norm_w[l].reshape(1, D_MODEL), w_in[l].astype(_bf16), qnw2, knw2, gsum,
            batch=batch, seq=seq, tm=T)
        lamv = jnp.stack([lambda_q1[l], lambda_k1[l], lambda_q2[l], lambda_k2[l]]).astype(_f32)
        bound = (_BOUND_SLACK * math.sqrt(DA_QK_DIM) * jnp.max(jnp.abs(q_norm_w[l]))
                 * jnp.max(jnp.abs(k_norm_w[l]))).astype(_f32)
        attn = functools.partial(_attn, batch=batch, seq=seq, T=T, lambda_init=lambda_init)
        oa = lax.cond(bound <= _MAX_FIXED_SHIFT,
                      functools.partial(attn, shifted=True), functools.partial(attn, shifted=False),
                      qT, kaug, vT, lamv, (-bound).reshape(1, 1))
        dec = jnp.zeros((2, 128), _f32)
        dec = dec.at[0, :RET_HEADS].set(ret_decay_fwd[l]).at[1, :RET_HEADS].set(ret_decay_bwd[l])
        orr = _retention(qr.reshape(batch, seq, 256), kr.reshape(batch, seq, 256),
                         vr.reshape(batch, seq, RET_WIDTH), dec, batch=batch, seq=seq, C=C)
        x2d = _outproj(x2d, oa.reshape(rows, DA_WIDTH), orr.reshape(rows, RET_WIDTH), gates,
                       subln_w[l].reshape(1, DA_V_DIM), w_out[l].astype(_bf16),
                       tm=T, lambda_init=lambda_init)
    return x2d.reshape(batch, seq, D_MODEL)
```

```python
import functools
import math

import jax
import jax.numpy as jnp
from jax import lax
from jax.experimental import pallas as pl
from jax.experimental.pallas import tpu as pltpu

D_MODEL = 1024
DA_HEADS = 4
DA_QK_DIM = 64
DA_V_DIM = 128
DA_WIDTH = DA_HEADS * DA_V_DIM
RET_HEADS = 4
RET_QK_DIM = 64
RET_V_DIM = 128
RET_WIDTH = RET_HEADS * RET_V_DIM
IN_WIDTH = 3584
EPS = 1e-6

_QA, _KA, _VA, _GA, _QR, _KR, _VR, _GR = 0, 512, 1024, 1536, 2048, 2304, 2560, 3072

_POS_SPLIT = 256
_KAUG_LANES = 256
_VT_ROWS = 144
_VMEM_LIMIT_BYTES = 56 * 1024 * 1024
_ROW_SPLITS = 2
_TILES_PER_GROUP = 16
_MAX_FIXED_SHIFT = 40.0
_BOUND_SLACK = 1.02

_f32 = jnp.float32
_bf16 = jnp.bfloat16


def _alibi_slope(head):
    return 2.0 ** (-8.0 * (head + 1) / DA_HEADS)


def _inproj_body(x_ref, nw_ref, w_ref, qnw_ref, knw_ref, gsum_ref,
                 qT_ref, kaug_ref, vT_ref, gate_ref, qr_ref, kr_ref, vr_ref, *, tm, tiles_per_seq):
    gsum = gsum_ref[...]

    def group_rms(y, w):
        y2 = y * y
        hi = y2.astype(_bf16)
        lo = (y2 - hi.astype(_f32)).astype(_bf16)
        ss = (jnp.dot(hi, gsum, preferred_element_type=_f32)
              + jnp.dot(lo, gsum, preferred_element_type=_f32))
        return y * lax.rsqrt(ss * (1.0 / DA_QK_DIM) + EPS) * w

    st = pl.program_id(0) % tiles_per_seq
    hm = tm // _ROW_SPLITS
    scale = DA_QK_DIM ** -0.5
    vrow = lax.broadcasted_iota(jnp.int32, (_VT_ROWS - DA_V_DIM, hm), 0)
    ones_rows = jnp.where(vrow == 0, 1.0, 0.0).astype(_bf16)
    lane = lax.broadcasted_iota(jnp.int32, (hm, 128), 1)

    for r in range(_ROW_SPLITS):
        rs = slice(r * hm, (r + 1) * hm)
        x = x_ref[rs, :]
        ms = jnp.mean(x * x, axis=-1, keepdims=True)
        xn = (x * lax.rsqrt(ms + EPS) * nw_ref[...]).astype(_bf16)
        h = jnp.dot(xn, w_ref[...], preferred_element_type=_f32)

        row = lax.broadcasted_iota(jnp.int32, (hm, 128), 0) + (st * tm + r * hm)
        pos_lo = (row % _POS_SPLIT).astype(_f32)
        pos_hi = ((row // _POS_SPLIT) * _POS_SPLIT).astype(_f32)

        for hd in range(DA_HEADS):
            c = 128 * hd
            q = group_rms(h[:, _QA + c:_QA + c + 128], qnw_ref[...]) * scale
            qT_ref[0, hd, :, rs] = q.T.astype(_bf16)
            k = group_rms(h[:, _KA + c:_KA + c + 128], knw_ref[...])
            kaug_ref[0, hd, rs, 0:128] = k.astype(_bf16)
            slope = _alibi_slope(hd)
            kpos = jnp.where(lane == 0, slope * pos_lo,
                             jnp.where(lane == 1, slope * pos_hi,
                                       jnp.where(lane < 5, 1.0, 0.0)))
            kaug_ref[0, hd, rs, 128:256] = kpos.astype(_bf16)
            v = h[:, _VA + c:_VA + c + 128]
            vT_ref[0, hd, 0, 0:DA_V_DIM, rs] = v.T.astype(_bf16)
            vT_ref[0, hd, 0, DA_V_DIM:_VT_ROWS, rs] = ones_rows

        ga = h[:, _GA:_GA + DA_WIDTH]
        gr = h[:, _GR:_GR + RET_WIDTH]
        gate_ref[rs, 0:DA_WIDTH] = (ga * jax.nn.sigmoid(ga)).astype(_bf16)
        gate_ref[rs, DA_WIDTH:DA_WIDTH + RET_WIDTH] = (gr * jax.nn.sigmoid(gr)).astype(_bf16)
        qr_ref[rs, :] = h[:, _QR:_QR + 256].astype(_bf16)
        kr_ref[rs, :] = (h[:, _KR:_KR + 256] * (RET_QK_DIM ** -0.5)).astype(_bf16)
        vr_ref[rs, :] = h[:, _VR:_VR + RET_WIDTH].astype(_bf16)


def _inproj(x2d, norm_w, w_in_bf16, qnw2, knw2, gsum, *, batch, seq, tm):
    rows = batch * seq
    tiles_per_seq = seq // tm
    nk = seq // tm
    full = lambda shape: pl.BlockSpec(shape, lambda i: (0,) * len(shape))
    b_of = lambda i: i // tiles_per_seq
    s_of = lambda i: i % tiles_per_seq
    out_shape = (
        jax.ShapeDtypeStruct((batch, DA_HEADS, 128, seq), _bf16),
        jax.ShapeDtypeStruct((batch, DA_HEADS, seq, _KAUG_LANES), _bf16),
        jax.ShapeDtypeStruct((batch, DA_HEADS, nk, _VT_ROWS, tm), _bf16),
        jax.ShapeDtypeStruct((rows, DA_WIDTH + RET_WIDTH), _bf16),
        jax.ShapeDtypeStruct((rows, 256), _bf16),
        jax.ShapeDtypeStruct((rows, 256), _bf16),
        jax.ShapeDtypeStruct((rows, RET_WIDTH), _bf16),
    )
    out_specs = (
        pl.BlockSpec((1, DA_HEADS, 128, tm), lambda i: (b_of(i), 0, 0, s_of(i))),
        pl.BlockSpec((1, DA_HEADS, tm, _KAUG_LANES), lambda i: (b_of(i), 0, s_of(i), 0)),
        pl.BlockSpec((1, DA_HEADS, 1, _VT_ROWS, tm), lambda i: (b_of(i), 0, s_of(i), 0, 0)),
        pl.BlockSpec((tm, DA_WIDTH + RET_WIDTH), lambda i: (i, 0)),
        pl.BlockSpec((tm, 256), lambda i: (i, 0)),
        pl.BlockSpec((tm, 256), lambda i: (i, 0)),
        pl.BlockSpec((tm, RET_WIDTH), lambda i: (i, 0)),
    )
    return pl.pallas_call(
        functools.partial(_inproj_body, tm=tm, tiles_per_seq=tiles_per_seq),
        grid=(rows // tm,),
        in_specs=[
            pl.BlockSpec((tm, D_MODEL), lambda i: (i, 0)),
            full((1, D_MODEL)),
            full((D_MODEL, IN_WIDTH)),
            full((1, 128)),
            full((1, 128)),
            full((128, 128)),
        ],
        out_specs=out_specs,
        out_shape=out_shape,
        compiler_params=pltpu.CompilerParams(
            dimension_semantics=("arbitrary",), vmem_limit_bytes=_VMEM_LIMIT_BYTES),
        name="inproj",
    )(x2d, norm_w, w_in_bf16, qnw2, knw2, gsum)


def _attn_setup(qT_ref, shift, qw_ref, acc_ref, d_ref, *, T):
    hd = pl.program_id(1)
    qi = pl.program_id(2)
    slope = jnp.where(hd == 0, _alibi_slope(0),
                      jnp.where(hd == 1, _alibi_slope(1),
                                jnp.where(hd == 2, _alibi_slope(2), _alibi_slope(3)))).astype(_f32)

    @pl.when(qi == 0)
    def _():
        jj = lax.broadcasted_iota(jnp.int32, (T, T), 0)
        ii = lax.broadcasted_iota(jnp.int32, (T, T), 1)
        d_ref[...] = (-2.0 * slope) * jnp.maximum(jj - ii, 0).astype(_f32)

    qt = qT_ref[0, 0]
    prow = lax.broadcasted_iota(jnp.int32, (16, T), 0)
    ipos = lax.broadcasted_iota(jnp.int32, (16, T), 1) + qi * T
    ilo = (ipos % _POS_SPLIT).astype(_f32)
    ihi = ((ipos // _POS_SPLIT) * _POS_SPLIT).astype(_f32)
    qpos = jnp.where(prow < 2, 1.0,
                     jnp.where(prow == 2, -slope * ilo,
                               jnp.where(prow == 3, -slope * ihi, 0.0)))
    shift_row = jnp.where(prow == 4, shift, 0.0)
    zeros64 = jnp.zeros((64, T), _bf16)
    for side in range(2):
        aug = ((qpos if side == 0 else -qpos) + shift_row).astype(_bf16)
        for mp in range(2):
            idx = 2 * side + mp
            qw_ref[idx, 0:64, :] = qt[0:64] if mp == 0 else zeros64
            qw_ref[idx, 64:128, :] = zeros64 if mp == 0 else qt[64:128]
            qw_ref[idx, 128:144, :] = aug
            qw_ref[idx, 144:_KAUG_LANES, :] = jnp.zeros((_KAUG_LANES - 144, T), _bf16)

    acc_ref[...] = jnp.zeros_like(acc_ref)


def _attn_finish(lamv_ref, acc_ref, o_ref, *, lambda_init):
    lv = lamv_ref[...]
    lam = (jnp.exp(jnp.sum(lv[0:1] * lv[1:2], axis=-1, keepdims=True))
           - jnp.exp(jnp.sum(lv[2:3] * lv[3:4], axis=-1, keepdims=True)) + lambda_init)
    a0 = acc_ref[0]
    a1 = acc_ref[1]
    inv0 = 1.0 / a0[DA_V_DIM:DA_V_DIM + 1]
    inv1 = lam / a1[DA_V_DIM:DA_V_DIM + 1]
    o = a0[0:DA_V_DIM] * inv0 - a1[0:DA_V_DIM] * inv1
    o_ref[0] = o.T.astype(o_ref.dtype)


def _attn_online_body(qT_ref, kaug_ref, vT_ref, lamv_ref, shift_ref, o_ref, qw_ref, acc_ref, d_ref, m_ref,
                      *, T, nk, lambda_init):
    del shift_ref
    qi = pl.program_id(2)
    _attn_setup(qT_ref, jnp.zeros((1, 1), _f32), qw_ref, acc_ref, d_ref, T=T)
    m_ref[...] = jnp.full(m_ref.shape, -1e30, _f32)

    def step(kt, side, diag):
        ktile = kaug_ref[0, 0, pl.ds(pl.multiple_of(kt * T, T), T), :]
        vtile = vT_ref[0, 0, kt]
        for mp in range(2):
            s = jnp.dot(ktile, qw_ref[2 * side + mp], preferred_element_type=_f32)
            if diag:
                s = s + d_ref[...]
            m_old = m_ref[mp]
            m_new = jnp.maximum(m_old, jnp.max(s, axis=0, keepdims=True))
            alpha = jnp.exp(m_old - m_new)
            p = jnp.exp(s - m_new).astype(_bf16)
            pv = jnp.dot(vtile, p, preferred_element_type=_f32)
            acc_ref[mp] = acc_ref[mp] * alpha + pv
            m_ref[mp] = m_new

    def before(kt, carry):
        step(kt, 0, False)
        return carry

    def after(kt, carry):
        step(kt, 1, False)
        return carry

    lax.fori_loop(0, qi, before, 0)
    step(qi, 0, True)
    lax.fori_loop(qi + 1, nk, after, 0)
    _attn_finish(lamv_ref, acc_ref, o_ref, lambda_init=lambda_init)


def _attn_shifted_body(qT_ref, kaug_ref, vT_ref, lamv_ref, shift_ref, o_ref, qw_ref, acc_ref, d_ref, s_ref,
                       *, T, nk, U, lambda_init):
    qi = pl.program_id(2)
    _attn_setup(qT_ref, shift_ref[...], qw_ref, acc_ref, d_ref, T=T)

    def key_tile(t):
        j = t - 1
        return jnp.where(t == 0, qi, j + (j >= qi).astype(jnp.int32))

    def scores(t, slot, diag=False):
        kt = key_tile(t)
        ktile = kaug_ref[0, 0, pl.ds(pl.multiple_of(kt * T, T), T), :]
        side = (kt > qi).astype(jnp.int32)
        for mp in range(2):
            s = jnp.dot(ktile, qw_ref[2 * side + mp], preferred_element_type=_f32)
            s_ref[slot, mp] = s + d_ref[...] if diag else s

    def consume(t, slot):
        vtile = vT_ref[0, 0, key_tile(t)]
        for mp in range(2):
            p = jnp.exp(s_ref[slot, mp]).astype(_bf16)
            acc_ref[mp] += jnp.dot(vtile, p, preferred_element_type=_f32)

    def group(g, last):
        for a in range(U):
            t = g * U + a
            if not (last and a == U - 1):
                scores(t + 1, (a + 1) % U % 3)
            consume(t, a % 3)

    def body(g, carry):
        group(g, False)
        return carry

    scores(0, 0, diag=True)
    n_groups = nk // U
    if n_groups > 1:
        lax.fori_loop(0, n_groups - 1, body, 0)
    group(n_groups - 1, True)
    _attn_finish(lamv_ref, acc_ref, o_ref, lambda_init=lambda_init)


def _attn(qT, kaug, vT, lamv, shift, *, batch, seq, T, lambda_init, shifted):
    nk = seq // T
    if shifted:
        tiles_per_group = next(u for u in (_TILES_PER_GROUP, 8, 5, 3, 2) if nk % u == 0 and u % 3 != 1)
        body = functools.partial(_attn_shifted_body, U=tiles_per_group)
        extra = pltpu.VMEM((3, 2, T, T), _f32)
    else:
        body = _attn_online_body
        extra = pltpu.VMEM((2, 1, T), _f32)
    return pl.pallas_call(
        functools.partial(body, T=T, nk=nk, lambda_init=lambda_init),
        grid=(batch, DA_HEADS, nk),
        in_specs=[
            pl.BlockSpec((1, 1, 128, T), lambda b, h, q: (b, h, 0, q)),
            pl.BlockSpec((1, 1, seq, _KAUG_LANES), lambda b, h, q: (b, h, 0, 0)),
            pl.BlockSpec((1, 1, nk, _VT_ROWS, T), lambda b, h, q: (b, h, 0, 0, 0)),
            pl.BlockSpec((4, DA_QK_DIM), lambda b, h, q: (0, 0)),
            pl.BlockSpec((1, 1), lambda b, h, q: (0, 0)),
        ],
        out_specs=pl.BlockSpec((1, T, DA_V_DIM), lambda b, h, q: (b, q, h)),
        out_shape=jax.ShapeDtypeStruct((batch, seq, DA_WIDTH), _bf16),
        scratch_shapes=[
            pltpu.VMEM((4, _KAUG_LANES, T), _bf16),
            pltpu.VMEM((2, _VT_ROWS, T), _f32),
            pltpu.VMEM((T, T), _f32),
            extra,
        ],
        compiler_params=pltpu.CompilerParams(
            dimension_semantics=("arbitrary", "arbitrary", "arbitrary"),
            vmem_limit_bytes=_VMEM_LIMIT_BYTES),
        name="diffattn_shifted" if shifted else "diffattn_online",
    )(qT, kaug, vT, lamv, shift)


def _log_sigmoid(x):
    return jnp.minimum(x, 0.0) - jnp.log1p(jnp.exp(-jnp.abs(x)))


def _ret_body(q_ref, k_ref, v_ref, dec_ref, o_ref, dm_ref, kvf_ref, kvb_ref, sf_ref, sb_ref, intra_ref,
              *, C, nc):
    hd = pl.program_id(1)
    lane = lax.broadcasted_iota(jnp.int32, (1, 128), 1)
    hmask = (lane // RET_QK_DIM) == (hd % 2)

    lg = _log_sigmoid(dec_ref[...])
    sel = lane == hd
    lgf = jnp.sum(jnp.where(sel, lg[0:1], 0.0), axis=-1, keepdims=True)
    lgb = jnp.sum(jnp.where(sel, lg[1:2], 0.0), axis=-1, keepdims=True)

    n_col = lax.broadcasted_iota(jnp.int32, (C, 1), 0).astype(_f32)
    tt = lax.broadcasted_iota(jnp.int32, (C, C), 0)
    ss = lax.broadcasted_iota(jnp.int32, (C, C), 1)
    diff = (tt - ss).astype(_f32)
    dm_ref[...] = jnp.where(diff >= 0.0, jnp.exp(lgf * jnp.maximum(diff, 0.0)),
                            jnp.exp(lgb * jnp.maximum(-diff, 0.0)))
    zeta_f = jnp.exp(lgf * (C - 1.0 - n_col))
    xi_f = jnp.exp(lgf * (n_col + 1.0))
    zeta_b = jnp.exp(lgb * (n_col + 1.0))
    xi_b = jnp.exp(lgb * (C - 1.0 - n_col))
    decay_f = jnp.exp(lgf * float(C))
    decay_b = jnp.exp(lgb * float(C))

    def rows_of(c):
        return pl.ds(pl.multiple_of(c * C, C), C)

    def intra(c, carry):
        rows = rows_of(c)
        q = jnp.where(hmask, q_ref[0, rows, :], jnp.zeros((), _bf16))
        k = k_ref[0, rows, :]
        v = v_ref[0, rows, :]
        a = lax.dot_general(q, k, (((1,), (1,)), ((), ())), preferred_element_type=_f32)
        a = (a * dm_ref[...]).astype(_bf16)
        intra_ref[rows, :] = jnp.dot(a, v, preferred_element_type=_f32)
        kf = (k.astype(_f32) * zeta_f).astype(_bf16)
        kb = (k.astype(_f32) * zeta_b).astype(_bf16)
        kvf_ref[c] = lax.dot_general(kf, v, (((0,), (0,)), ((), ())), preferred_element_type=_f32)
        kvb_ref[c] = lax.dot_general(kb, v, (((0,), (0,)), ((), ())), preferred_element_type=_f32)
        return carry

    lax.fori_loop(0, nc, intra, 0, unroll=4)

    def scan_f(c, state):
        sf_ref[c] = state.astype(_bf16)
        return decay_f * state + kvf_ref[c]

    def scan_b(i, state):
        c = nc - 1 - i
        sb_ref[c] = state.astype(_bf16)
        return decay_b * state + kvb_ref[c]

    zero_state = jnp.zeros((128, RET_V_DIM), _f32)
    lax.fori_loop(0, nc, scan_f, zero_state)
    lax.fori_loop(0, nc, scan_b, zero_state)

    def cross(c, carry):
        rows = rows_of(c)
        q = jnp.where(hmask, q_ref[0, rows, :], jnp.zeros((), _bf16)).astype(_f32)
        qf = (q * xi_f).astype(_bf16)
        qb = (q * xi_b).astype(_bf16)
        o_ref[0, rows, :] = (intra_ref[rows, :] + jnp.dot(qf, sf_ref[c], preferred_element_type=_f32)
                             + jnp.dot(qb, sb_ref[c], preferred_element_type=_f32)).astype(o_ref.dtype)
        return carry

    lax.fori_loop(0, nc, cross, 0, unroll=4)


def _retention(qr, kr, vr, dec, *, batch, seq, C):
    nc = seq // C
    return pl.pallas_call(
        functools.partial(_ret_body, C=C, nc=nc),
        grid=(batch, RET_HEADS),
        in_specs=[
            pl.BlockSpec((1, seq, 128), lambda b, h: (b, 0, h // 2)),
            pl.BlockSpec((1, seq, 128), lambda b, h: (b, 0, h // 2)),
            pl.BlockSpec((1, seq, RET_V_DIM), lambda b, h: (b, 0, h)),
            pl.BlockSpec((2, 128), lambda b, h: (0, 0)),
        ],
        out_specs=pl.BlockSpec((1, seq, RET_V_DIM), lambda b, h: (b, 0, h)),
        out_shape=jax.ShapeDtypeStruct((batch, seq, RET_WIDTH), _bf16),
        scratch_shapes=[
            pltpu.VMEM((C, C), _f32),
            pltpu.VMEM((nc, 128, RET_V_DIM), _f32),
            pltpu.VMEM((nc, 128, RET_V_DIM), _f32),
            pltpu.VMEM((nc, 128, RET_V_DIM), _bf16),
            pltpu.VMEM((nc, 128, RET_V_DIM), _bf16),
            pltpu.VMEM((seq, RET_V_DIM), _f32),
        ],
        compiler_params=pltpu.CompilerParams(
            dimension_semantics=("arbitrary", "arbitrary"), vmem_limit_bytes=_VMEM_LIMIT_BYTES),
        name="retention",
    )(qr, kr, vr, dec)


def _outproj_body(x_ref, oa_ref, or_ref, gate_ref, sw_ref, w_ref, o_ref, *, lambda_init):
    oa = oa_ref[...].astype(_f32)
    orr = or_ref[...].astype(_f32)
    pieces = []
    for hd in range(DA_HEADS):
        y = oa[:, 128 * hd:128 * hd + 128]
        y = y * lax.rsqrt(jnp.mean(y * y, axis=-1, keepdims=True) + EPS) * sw_ref[...]
        pieces.append(y * (1.0 - lambda_init))
    for hd in range(RET_HEADS):
        y = orr[:, 128 * hd:128 * hd + 128]
        pieces.append(y * lax.rsqrt(jnp.mean(y * y, axis=-1, keepdims=True) + EPS))
    mixed = jnp.concatenate(pieces, axis=-1) * gate_ref[...].astype(_f32)
    o_ref[...] = x_ref[...] + jnp.dot(mixed.astype(_bf16), w_ref[...], preferred_element_type=_f32)


def _outproj(x2d, oa2d, or2d, gates, subln_w, w_out_bf16, *, tm, lambda_init):
    rows = x2d.shape[0]
    return pl.pallas_call(
        functools.partial(_outproj_body, lambda_init=lambda_init),
        grid=(rows // tm,),
        in_specs=[
            pl.BlockSpec((tm, D_MODEL), lambda i: (i, 0)),
            pl.BlockSpec((tm, DA_WIDTH), lambda i: (i, 0)),
            pl.BlockSpec((tm, RET_WIDTH), lambda i: (i, 0)),
            pl.BlockSpec((tm, DA_WIDTH + RET_WIDTH), lambda i: (i, 0)),
            pl.BlockSpec((1, DA_V_DIM), lambda i: (0, 0)),
            pl.BlockSpec((DA_WIDTH + RET_WIDTH, D_MODEL), lambda i: (0, 0)),
        ],
        out_specs=pl.BlockSpec((tm, D_MODEL), lambda i: (i, 0)),
        out_shape=jax.ShapeDtypeStruct((rows, D_MODEL), _f32),
        compiler_params=pltpu.CompilerParams(
            dimension_semantics=("arbitrary",), vmem_limit_bytes=_VMEM_LIMIT_BYTES),
        name="outproj",
    )(x2d, oa2d, or2d, gates, subln_w, w_out_bf16)


def _tiles(seq):
    attn_tile = min(512, seq // 2)
    ret_chunk = min(256, seq // 2)
    return attn_tile, ret_chunk


def kernel(x, norm_w, w_in, q_norm_w, k_norm_w, lambda_q1, lambda_k1, lambda_q2, lambda_k2,
           subln_w, ret_decay_fwd, ret_decay_bwd, w_out):
    batch, seq, _ = x.shape
    depth = norm_w.shape[0]
    T, C = _tiles(seq)
    rows = batch * seq

    lane = jnp.arange(128)
    gsum = (lane[:, None] // DA_QK_DIM == lane[None, :] // DA_QK_DIM).astype(_bf16)

    x2d = x.reshape(rows, D_MODEL)
    for l in range(depth):
        lambda_init = 0.8 - 0.6 * math.exp(-0.3 * l)
        qnw2 = jnp.tile(q_norm_w[l], 2).reshape(1, 128)
        knw2 = jnp.tile(k_norm_w[l], 2).reshape(1, 128)
        qT, kaug, vT, gates, qr, kr, vr = _inproj(
            x2d, norm_w[l].reshape(1, D_MODEL), w_in[l].astype(_bf16), qnw2, knw2, gsum,
            batch=batch, seq=seq, tm=T)
        lamv = jnp.stack([lambda_q1[l], lambda_k1[l], lambda_q2[l], lambda_k2[l]]).astype(_f32)
        bound = (_BOUND_SLACK * math.sqrt(DA_QK_DIM) * jnp.max(jnp.abs(q_norm_w[l]))
                 * jnp.max(jnp.abs(k_norm_w[l]))).astype(_f32)
        attn = functools.partial(_attn, batch=batch, seq=seq, T=T, lambda_init=lambda_init)
        oa = lax.cond(bound <= _MAX_FIXED_SHIFT,
                      functools.partial(attn, shifted=True), functools.partial(attn, shifted=False),
                      qT, kaug, vT, lamv, (-bound).reshape(1, 1))
        dec = jnp.zeros((2, 128), _f32)
        dec = dec.at[0, :RET_HEADS].set(ret_decay_fwd[l]).at[1, :RET_HEADS].set(ret_decay_bwd[l])
        orr = _retention(qr.reshape(batch, seq, 256), kr.reshape(batch, seq, 256),
                         vr.reshape(batch, seq, RET_WIDTH), dec, batch=batch, seq=seq, C=C)
        x2d = _outproj(x2d, oa.reshape(rows, DA_WIDTH), orr.reshape(rows, RET_WIDTH), gates,
                       subln_w[l].reshape(1, DA_V_DIM), w_out[l].astype(_bf16),
                       tm=T, lambda_init=lambda_init)
    return x2d.reshape(batch, seq, D_MODEL)
```

```python
import functools
import math

import jax
import jax.numpy as jnp
from jax import lax
from jax.experimental import pallas as pl
from jax.experimental.pallas import tpu as pltpu

D_MODEL = 1024
DA_HEADS = 4
DA_QK_DIM = 64
DA_V_DIM = 128
DA_WIDTH = DA_HEADS * DA_V_DIM
RET_HEADS = 4
RET_QK_DIM = 64
RET_V_DIM = 128
RET_WIDTH = RET_HEADS * RET_V_DIM
IN_WIDTH = 3584
EPS = 1e-6

_QA, _KA, _VA, _GA, _QR, _KR, _VR, _GR = 0, 512, 1024, 1536, 2048, 2304, 2560, 3072

_POS_SPLIT = 256
_KAUG_LANES = 256
_VT_ROWS = 144
_VMEM_LIMIT_BYTES = 56 * 1024 * 1024
_ROW_SPLITS = 2
_TILES_PER_GROUP = 16
_RET_UNROLL = 16
_MAX_FIXED_SHIFT = 40.0
_BOUND_SLACK = 1.02

_f32 = jnp.float32
_bf16 = jnp.bfloat16


def _alibi_slope(head):
    return 2.0 ** (-8.0 * (head + 1) / DA_HEADS)


def _inproj_body(x_ref, nw_ref, w_ref, qnw_ref, knw_ref, gsum_ref,
                 qT_ref, kaug_ref, vT_ref, gate_ref, qr_ref, kr_ref, vr_ref, *, tm, tiles_per_seq):
    gsum = gsum_ref[...]

    def group_rms(y, w):
        y2 = y * y
        hi = y2.astype(_bf16)
        lo = (y2 - hi.astype(_f32)).astype(_bf16)
        ss = (jnp.dot(hi, gsum, preferred_element_type=_f32)
              + jnp.dot(lo, gsum, preferred_element_type=_f32))
        return y * lax.rsqrt(ss * (1.0 / DA_QK_DIM) + EPS) * w

    st = pl.program_id(0) % tiles_per_seq
    hm = tm // _ROW_SPLITS
    scale = DA_QK_DIM ** -0.5
    vrow = lax.broadcasted_iota(jnp.int32, (_VT_ROWS - DA_V_DIM, hm), 0)
    ones_rows = jnp.where(vrow == 0, 1.0, 0.0).astype(_bf16)
    lane = lax.broadcasted_iota(jnp.int32, (hm, 128), 1)

    for r in range(_ROW_SPLITS):
        rs = slice(r * hm, (r + 1) * hm)
        x = x_ref[rs, :]
        ms = jnp.mean(x * x, axis=-1, keepdims=True)
        xn = (x * lax.rsqrt(ms + EPS) * nw_ref[...]).astype(_bf16)
        h = jnp.dot(xn, w_ref[...], preferred_element_type=_f32)

        row = lax.broadcasted_iota(jnp.int32, (hm, 128), 0) + (st * tm + r * hm)
        pos_lo = (row % _POS_SPLIT).astype(_f32)
        pos_hi = ((row // _POS_SPLIT) * _POS_SPLIT).astype(_f32)

        for hd in range(DA_HEADS):
            c = 128 * hd
            q = group_rms(h[:, _QA + c:_QA + c + 128], qnw_ref[...]) * scale
            qT_ref[0, hd, :, rs] = q.T.astype(_bf16)
            k = group_rms(h[:, _KA + c:_KA + c + 128], knw_ref[...])
            kaug_ref[0, hd, rs, 0:128] = k.astype(_bf16)
            slope = _alibi_slope(hd)
            kpos = jnp.where(lane == 0, slope * pos_lo,
                             jnp.where(lane == 1, slope * pos_hi,
                                       jnp.where(lane < 5, 1.0, 0.0)))
            kaug_ref[0, hd, rs, 128:256] = kpos.astype(_bf16)
            v = h[:, _VA + c:_VA + c + 128]
            vT_ref[0, hd, 0, 0:DA_V_DIM, rs] = v.T.astype(_bf16)
            vT_ref[0, hd, 0, DA_V_DIM:_VT_ROWS, rs] = ones_rows

        ga = h[:, _GA:_GA + DA_WIDTH]
        gr = h[:, _GR:_GR + RET_WIDTH]
        gate_ref[rs, 0:DA_WIDTH] = (ga * jax.nn.sigmoid(ga)).astype(_bf16)
        gate_ref[rs, DA_WIDTH:DA_WIDTH + RET_WIDTH] = (gr * jax.nn.sigmoid(gr)).astype(_bf16)
        qr_ref[rs, :] = h[:, _QR:_QR + 256].astype(_bf16)
        kr_ref[rs, :] = (h[:, _KR:_KR + 256] * (RET_QK_DIM ** -0.5)).astype(_bf16)
        vr_ref[rs, :] = h[:, _VR:_VR + RET_WIDTH].astype(_bf16)


def _inproj(x2d, norm_w, w_in_bf16, qnw2, knw2, gsum, *, batch, seq, tm):
    rows = batch * seq
    tiles_per_seq = seq // tm
    nk = seq // tm
    full = lambda shape: pl.BlockSpec(shape, lambda i: (0,) * len(shape))
    b_of = lambda i: i // tiles_per_seq
    s_of = lambda i: i % tiles_per_seq
    out_shape = (
        jax.ShapeDtypeStruct((batch, DA_HEADS, 128, seq), _bf16),
        jax.ShapeDtypeStruct((batch, DA_HEADS, seq, _KAUG_LANES), _bf16),
        jax.ShapeDtypeStruct((batch, DA_HEADS, nk, _VT_ROWS, tm), _bf16),
        jax.ShapeDtypeStruct((rows, DA_WIDTH + RET_WIDTH), _bf16),
        jax.ShapeDtypeStruct((rows, 256), _bf16),
        jax.ShapeDtypeStruct((rows, 256), _bf16),
        jax.ShapeDtypeStruct((rows, RET_WIDTH), _bf16),
    )
    out_specs = (
        pl.BlockSpec((1, DA_HEADS, 128, tm), lambda i: (b_of(i), 0, 0, s_of(i))),
        pl.BlockSpec((1, DA_HEADS, tm, _KAUG_LANES), lambda i: (b_of(i), 0, s_of(i), 0)),
        pl.BlockSpec((1, DA_HEADS, 1, _VT_ROWS, tm), lambda i: (b_of(i), 0, s_of(i), 0, 0)),
        pl.BlockSpec((tm, DA_WIDTH + RET_WIDTH), lambda i: (i, 0)),
        pl.BlockSpec((tm, 256), lambda i: (i, 0)),
        pl.BlockSpec((tm, 256), lambda i: (i, 0)),
        pl.BlockSpec((tm, RET_WIDTH), lambda i: (i, 0)),
    )
    return pl.pallas_call(
        functools.partial(_inproj_body, tm=tm, tiles_per_seq=tiles_per_seq),
        grid=(rows // tm,),
        in_specs=[
            pl.BlockSpec((tm, D_MODEL), lambda i: (i, 0)),
            full((1, D_MODEL)),
            full((D_MODEL, IN_WIDTH)),
            full((1, 128)),
            full((1, 128)),
            full((128, 128)),
        ],
        out_specs=out_specs,
        out_shape=out_shape,
        compiler_params=pltpu.CompilerParams(
            dimension_semantics=("arbitrary",), vmem_limit_bytes=_VMEM_LIMIT_BYTES),
        name="inproj",
    )(x2d, norm_w, w_in_bf16, qnw2, knw2, gsum)


def _attn_setup(qT_ref, shift, qw_ref, acc_ref, d_ref, *, T, Tq):
    hd = pl.program_id(1)
    qi = pl.program_id(2)
    slope = jnp.where(hd == 0, _alibi_slope(0),
                      jnp.where(hd == 1, _alibi_slope(1),
                                jnp.where(hd == 2, _alibi_slope(2), _alibi_slope(3)))).astype(_f32)

    @pl.when(qi == 0)
    def _():
        jj = lax.broadcasted_iota(jnp.int32, (T, Tq), 0)
        ii = lax.broadcasted_iota(jnp.int32, (T, Tq), 1)
        for r in range(Tq // T):
            d_ref[r] = (-2.0 * slope) * jnp.maximum(jj + r * T - ii, 0).astype(_f32)

    qt = qT_ref[0, 0]
    prow = lax.broadcasted_iota(jnp.int32, (16, Tq), 0)
    ipos = lax.broadcasted_iota(jnp.int32, (16, Tq), 1) + qi * Tq
    ilo = (ipos % _POS_SPLIT).astype(_f32)
    ihi = ((ipos // _POS_SPLIT) * _POS_SPLIT).astype(_f32)
    qpos = jnp.where(prow < 2, 1.0,
                     jnp.where(prow == 2, -slope * ilo,
                               jnp.where(prow == 3, -slope * ihi, 0.0)))
    shift_row = jnp.where(prow == 4, shift, 0.0)
    zeros64 = jnp.zeros((64, Tq), _bf16)
    for side in range(2):
        aug = ((qpos if side == 0 else -qpos) + shift_row).astype(_bf16)
        for mp in range(2):
            idx = 2 * side + mp
            qw_ref[idx, 0:64, :] = qt[0:64] if mp == 0 else zeros64
            qw_ref[idx, 64:128, :] = zeros64 if mp == 0 else qt[64:128]
            qw_ref[idx, 128:144, :] = aug
            qw_ref[idx, 144:_KAUG_LANES, :] = jnp.zeros((_KAUG_LANES - 144, Tq), _bf16)

    acc_ref[...] = jnp.zeros_like(acc_ref)


def _attn_finish(lamv_ref, acc_ref, o_ref, *, lambda_init):
    lv = lamv_ref[...]
    lam = (jnp.exp(jnp.sum(lv[0:1] * lv[1:2], axis=-1, keepdims=True))
           - jnp.exp(jnp.sum(lv[2:3] * lv[3:4], axis=-1, keepdims=True)) + lambda_init)
    a0 = acc_ref[0]
    a1 = acc_ref[1]
    inv0 = 1.0 / a0[DA_V_DIM:DA_V_DIM + 1]
    inv1 = lam / a1[DA_V_DIM:DA_V_DIM + 1]
    o = a0[0:DA_V_DIM] * inv0 - a1[0:DA_V_DIM] * inv1
    o_ref[0] = o.T.astype(o_ref.dtype)


def _attn_online_body(qT_ref, kaug_ref, vT_ref, lamv_ref, shift_ref, o_ref, qw_ref, acc_ref, d_ref, m_ref,
                      *, T, Tq, nk, lambda_init):
    del shift_ref
    qi = pl.program_id(2)
    R = Tq // T
    _attn_setup(qT_ref, jnp.zeros((1, 1), _f32), qw_ref, acc_ref, d_ref, T=T, Tq=Tq)
    m_ref[...] = jnp.full(m_ref.shape, -1e30, _f32)

    def step(kt, side, diag):
        ktile = kaug_ref[0, 0, pl.ds(pl.multiple_of(kt * T, T), T), :]
        vtile = vT_ref[0, 0, kt]
        for mp in range(2):
            s = jnp.dot(ktile, qw_ref[2 * side + mp], preferred_element_type=_f32)
            if diag is not None:
                s = s + d_ref[diag]
            m_old = m_ref[mp]
            m_new = jnp.maximum(m_old, jnp.max(s, axis=0, keepdims=True))
            alpha = jnp.exp(m_old - m_new)
            p = jnp.exp(s - m_new).astype(_bf16)
            pv = jnp.dot(vtile, p, preferred_element_type=_f32)
            acc_ref[mp] = acc_ref[mp] * alpha + pv
            m_ref[mp] = m_new

    def before(kt, carry):
        step(kt, 0, None)
        return carry

    def after(kt, carry):
        step(kt, 1, None)
        return carry

    lax.fori_loop(0, R * qi, before, 0)
    for r in range(R):
        step(R * qi + r, 0, r)
    lax.fori_loop(R * qi + R, nk, after, 0)
    _attn_finish(lamv_ref, acc_ref, o_ref, lambda_init=lambda_init)


def _attn_shifted_body(qT_ref, kaug_ref, vT_ref, lamv_ref, shift_ref, o_ref, qw_ref, acc_ref, d_ref, s_ref,
                       *, T, Tq, nk, U, lambda_init):
    qi = pl.program_id(2)
    R = Tq // T
    _attn_setup(qT_ref, shift_ref[...], qw_ref, acc_ref, d_ref, T=T, Tq=Tq)

    def key_tile(t):
        j = t - R
        return jnp.where(t < R, R * qi + t, j + R * (j >= R * qi).astype(jnp.int32))

    def scores(t, slot, diag=None):
        kt = key_tile(t)
        ktile = kaug_ref[0, 0, pl.ds(pl.multiple_of(kt * T, T), T), :]
        side = (kt >= R * qi + R).astype(jnp.int32)
        for mp in range(2):
            s = jnp.dot(ktile, qw_ref[2 * side + mp], preferred_element_type=_f32)
            s_ref[slot, mp] = s if diag is None else s + d_ref[diag]

    def consume(t, slot):
        vtile = vT_ref[0, 0, key_tile(t)]
        for mp in range(2):
            p = jnp.exp(s_ref[slot, mp]).astype(_bf16)
            acc_ref[mp] += jnp.dot(vtile, p, preferred_element_type=_f32)

    def group(g, last):
        for a in range(U):
            t = g * U + a
            if not (last and a == U - 1):
                nxt = t + 1
                scores(nxt, (a + 1) % U % 3, diag=nxt if isinstance(nxt, int) and nxt < R else None)
            consume(t, a % 3)

    def body(g, carry):
        group(g, False)
        return carry

    assert U >= R
    scores(0, 0, diag=0)
    n_groups = nk // U
    if n_groups > 1:
        group(0, False)
        lax.fori_loop(1, n_groups - 1, body, 0)
    group(n_groups - 1, True)
    _attn_finish(lamv_ref, acc_ref, o_ref, lambda_init=lambda_init)


def _attn(qT, kaug, vT, lamv, shift, *, batch, seq, T, Tq, lambda_init, shifted):
    nk = seq // T
    if shifted:
        tiles_per_group = next(u for u in (_TILES_PER_GROUP, 8, 5, 3, 2) if nk % u == 0 and u % 3 != 1)
        body = functools.partial(_attn_shifted_body, U=tiles_per_group)
        extra = pltpu.VMEM((3, 2, T, Tq), _f32)
    else:
        body = _attn_online_body
        extra = pltpu.VMEM((2, 1, Tq), _f32)
    return pl.pallas_call(
        functools.partial(body, T=T, Tq=Tq, nk=nk, lambda_init=lambda_init),
        grid=(batch, DA_HEADS, seq // Tq),
        in_specs=[
            pl.BlockSpec((1, 1, 128, Tq), lambda b, h, q: (b, h, 0, q)),
            pl.BlockSpec((1, 1, seq, _KAUG_LANES), lambda b, h, q: (b, h, 0, 0)),
            pl.BlockSpec((1, 1, nk, _VT_ROWS, T), lambda b, h, q: (b, h, 0, 0, 0)),
            pl.BlockSpec((4, DA_QK_DIM), lambda b, h, q: (0, 0)),
            pl.BlockSpec((1, 1), lambda b, h, q: (0, 0)),
        ],
        out_specs=pl.BlockSpec((1, Tq, DA_V_DIM), lambda b, h, q: (b, q, h)),
        out_shape=jax.ShapeDtypeStruct((batch, seq, DA_WIDTH), _bf16),
        scratch_shapes=[
            pltpu.VMEM((4, _KAUG_LANES, Tq), _bf16),
            pltpu.VMEM((2, _VT_ROWS, Tq), _f32),
            pltpu.VMEM((Tq // T, T, Tq), _f32),
            extra,
        ],
        compiler_params=pltpu.CompilerParams(
            dimension_semantics=("arbitrary", "arbitrary", "arbitrary"),
            vmem_limit_bytes=_VMEM_LIMIT_BYTES),
        name="diffattn_shifted" if shifted else "diffattn_online",
    )(qT, kaug, vT, lamv, shift)


def _log_sigmoid(x):
    return jnp.minimum(x, 0.0) - jnp.log1p(jnp.exp(-jnp.abs(x)))


def _ret_body(q_ref, k_ref, v_ref, dec_ref, o_ref, dm_ref, kvf_ref, kvb_ref, sfb_ref, intra_ref,
              *, C, nc):
    hd = pl.program_id(1)
    lane = lax.broadcasted_iota(jnp.int32, (1, 128), 1)
    hmask = (lane // RET_QK_DIM) == (hd % 2)

    lg = _log_sigmoid(dec_ref[...])
    sel = lane == hd
    lgf = jnp.sum(jnp.where(sel, lg[0:1], 0.0), axis=-1, keepdims=True)
    lgb = jnp.sum(jnp.where(sel, lg[1:2], 0.0), axis=-1, keepdims=True)

    n_col = lax.broadcasted_iota(jnp.int32, (C, 1), 0).astype(_f32)
    tt = lax.broadcasted_iota(jnp.int32, (C, C), 0)
    ss = lax.broadcasted_iota(jnp.int32, (C, C), 1)
    diff = (tt - ss).astype(_f32)
    dm_ref[...] = jnp.where(diff >= 0.0, jnp.exp(lgf * jnp.maximum(diff, 0.0)),
                            jnp.exp(lgb * jnp.maximum(-diff, 0.0)))
    zeta_f = jnp.exp(lgf * (C - 1.0 - n_col))
    xi_f = jnp.exp(lgf * (n_col + 1.0))
    zeta_b = jnp.exp(lgb * (n_col + 1.0))
    xi_b = jnp.exp(lgb * (C - 1.0 - n_col))
    decay_f = jnp.exp(lgf * float(C))
    decay_b = jnp.exp(lgb * float(C))

    def rows_of(c):
        return pl.ds(pl.multiple_of(c * C, C), C)

    def intra(c, carry):
        rows = rows_of(c)
        q = jnp.where(hmask, q_ref[0, rows, :], jnp.zeros((), _bf16))
        k = k_ref[0, rows, :]
        v = v_ref[0, rows, :]
        a = lax.dot_general(q, k, (((1,), (1,)), ((), ())), preferred_element_type=_f32)
        a = (a * dm_ref[...]).astype(_bf16)
        intra_ref[rows, :] = jnp.dot(a, v, preferred_element_type=_f32)
        kf = (k.astype(_f32) * zeta_f).astype(_bf16)
        kb = (k.astype(_f32) * zeta_b).astype(_bf16)
        kvf_ref[c] = lax.dot_general(kf, v, (((0,), (0,)), ((), ())), preferred_element_type=_f32)
        kvb_ref[c] = lax.dot_general(kb, v, (((0,), (0,)), ((), ())), preferred_element_type=_f32)
        return carry

    lax.fori_loop(0, nc, intra, 0, unroll=min(_RET_UNROLL, nc))

    def scan_f(c, state):
        sfb_ref[c, 0:128, :] = state.astype(_bf16)
        return decay_f * state + kvf_ref[c]

    def scan_b(i, state):
        c = nc - 1 - i
        sfb_ref[c, 128:256, :] = state.astype(_bf16)
        return decay_b * state + kvb_ref[c]

    zero_state = jnp.zeros((128, RET_V_DIM), _f32)
    lax.fori_loop(0, nc, scan_f, zero_state)
    lax.fori_loop(0, nc, scan_b, zero_state)

    def cross(c, carry):
        rows = rows_of(c)
        q = jnp.where(hmask, q_ref[0, rows, :], jnp.zeros((), _bf16)).astype(_f32)
        qfb = jnp.concatenate([q * xi_f, q * xi_b], axis=1).astype(_bf16)
        o_ref[0, rows, :] = (intra_ref[rows, :]
                             + jnp.dot(qfb, sfb_ref[c], preferred_element_type=_f32)).astype(o_ref.dtype)
        return carry

    lax.fori_loop(0, nc, cross, 0, unroll=min(_RET_UNROLL, nc))


def _retention(qr, kr, vr, dec, *, batch, seq, C):
    nc = seq // C
    return pl.pallas_call(
        functools.partial(_ret_body, C=C, nc=nc),
        grid=(batch, RET_HEADS),
        in_specs=[
            pl.BlockSpec((1, seq, 128), lambda b, h: (b, 0, h // 2)),
            pl.BlockSpec((1, seq, 128), lambda b, h: (b, 0, h // 2)),
            pl.BlockSpec((1, seq, RET_V_DIM), lambda b, h: (b, 0, h)),
            pl.BlockSpec((2, 128), lambda b, h: (0, 0)),
        ],
        out_specs=pl.BlockSpec((1, seq, RET_V_DIM), lambda b, h: (b, 0, h)),
        out_shape=jax.ShapeDtypeStruct((batch, seq, RET_WIDTH), _bf16),
        scratch_shapes=[
            pltpu.VMEM((C, C), _f32),
            pltpu.VMEM((nc, 128, RET_V_DIM), _f32),
            pltpu.VMEM((nc, 128, RET_V_DIM), _f32),
            pltpu.VMEM((nc, 256, RET_V_DIM), _bf16),
            pltpu.VMEM((seq, RET_V_DIM), _f32),
        ],
        compiler_params=pltpu.CompilerParams(
            dimension_semantics=("arbitrary", "arbitrary"), vmem_limit_bytes=_VMEM_LIMIT_BYTES),
        name="retention",
    )(qr, kr, vr, dec)


def _outproj_body(x_ref, oa_ref, or_ref, gate_ref, sw_ref, w_ref, o_ref, *, lambda_init):
    oa = oa_ref[...].astype(_f32)
    orr = or_ref[...].astype(_f32)
    pieces = []
    for hd in range(DA_HEADS):
        y = oa[:, 128 * hd:128 * hd + 128]
        y = y * lax.rsqrt(jnp.mean(y * y, axis=-1, keepdims=True) + EPS) * sw_ref[...]
        pieces.append(y * (1.0 - lambda_init))
    for hd in range(RET_HEADS):
        y = orr[:, 128 * hd:128 * hd + 128]
        pieces.append(y * lax.rsqrt(jnp.mean(y * y, axis=-1, keepdims=True) + EPS))
    mixed = jnp.concatenate(pieces, axis=-1) * gate_ref[...].astype(_f32)
    o_ref[...] = x_ref[...] + jnp.dot(mixed.astype(_bf16), w_ref[...], preferred_element_type=_f32)


def _outproj(x2d, oa2d, or2d, gates, subln_w, w_out_bf16, *, tm, lambda_init):
    rows = x2d.shape[0]
    return pl.pallas_call(
        functools.partial(_outproj_body, lambda_init=lambda_init),
        grid=(rows // tm,),
        in_specs=[
            pl.BlockSpec((tm, D_MODEL), lambda i: (i, 0)),
            pl.BlockSpec((tm, DA_WIDTH), lambda i: (i, 0)),
            pl.BlockSpec((tm, RET_WIDTH), lambda i: (i, 0)),
            pl.BlockSpec((tm, DA_WIDTH + RET_WIDTH), lambda i: (i, 0)),
            pl.BlockSpec((1, DA_V_DIM), lambda i: (0, 0)),
            pl.BlockSpec((DA_WIDTH + RET_WIDTH, D_MODEL), lambda i: (0, 0)),
        ],
        out_specs=pl.BlockSpec((tm, D_MODEL), lambda i: (i, 0)),
        out_shape=jax.ShapeDtypeStruct((rows, D_MODEL), _f32),
        compiler_params=pltpu.CompilerParams(
            dimension_semantics=("arbitrary",), vmem_limit_bytes=_VMEM_LIMIT_BYTES),
        name="outproj",
    )(x2d, oa2d, or2d, gates, subln_w, w_out_bf16)


def _tiles(seq):
    key_tile = min(512, seq // 2)
    query_tile = min(1024, seq // 2)
    ret_chunk = min(256, seq // 2)
    return key_tile, query_tile, ret_chunk


def kernel(x, norm_w, w_in, q_norm_w, k_norm_w, lambda_q1, lambda_k1, lambda_q2, lambda_k2,
           subln_w, ret_decay_fwd, ret_decay_bwd, w_out):
    batch, seq, _ = x.shape
    depth = norm_w.shape[0]
    T, Tq, C = _tiles(seq)
    rows = batch * seq

    lane = jnp.arange(128)
    gsum = (lane[:, None] // DA_QK_DIM == lane[None, :] // DA_QK_DIM).astype(_bf16)

    x2d = x.reshape(rows, D_MODEL)
    for l in range(depth):
        lambda_init = 0.8 - 0.6 * math.exp(-0.3 * l)
        qnw2 = jnp.tile(q_norm_w[l], 2).reshape(1, 128)
        knw2 = jnp.tile(k_norm_w[l], 2).reshape(1, 128)
        qT, kaug, vT, gates, qr, kr, vr = _inproj(
            x2d, norm_w[l].reshape(1, D_MODEL), w_in[l].astype(_bf16), qnw2, knw2, gsum,
            batch=batch, seq=seq, tm=T)
        lamv = jnp.stack([lambda_q1[l], lambda_k1[l], lambda_q2[l], lambda_k2[l]]).astype(_f32)
        bound = (_BOUND_SLACK * math.sqrt(DA_QK_DIM) * jnp.max(jnp.abs(q_norm_w[l]))
                 * jnp.max(jnp.abs(k_norm_w[l]))).astype(_f32)
        attn = functools.partial(_attn, batch=batch, seq=seq, T=T, Tq=Tq, lambda_init=lambda_init)
        oa = lax.cond(bound <= _MAX_FIXED_SHIFT,
                      functools.partial(attn, shifted=True), functools.partial(attn, shifted=False),
                      qT, kaug, vT, lamv, (-bound).reshape(1, 1))
        dec = jnp.zeros((2, 128), _f32)
        dec = dec.at[0, :RET_HEADS].set(ret_decay_fwd[l]).at[1, :RET_HEADS].set(ret_decay_bwd[l])
        orr = _retention(qr.reshape(batch, seq, 256), kr.reshape(batch, seq, 256),
                         vr.reshape(batch, seq, RET_WIDTH), dec, batch=batch, seq=seq, C=C)
        x2d = _outproj(x2d, oa.reshape(rows, DA_WIDTH), orr.reshape(rows, RET_WIDTH), gates,
                       subln_w[l].reshape(1, DA_V_DIM), w_out[l].astype(_bf16),
                       tm=2 * T, lambda_init=lambda_init)
    return x2d.reshape(batch, seq, D_MODEL)
```
